```python
import math
import jax, jax.numpy as jnp
from jax import lax
import numpy as np

D_MODEL = 1024
BATCH = 16
SEQ = 2048
DEPTH = 1
DEC_BATCH = 32
DEC_SEQ = 8
PAST_LEN = 16384
PAGE_SIZE = 128

N_HEADS = 16
HEAD_DIM = D_MODEL // N_HEADS
D_ATTN = N_HEADS * HEAD_DIM
D_CONV = D_MODEL
CONV_W = 3
PLE_DIM = 256
Q_BLOCK = 128
EPS = 1e-6
SB_BIAS_INIT = -9.0
SPLITS = (D_CONV, D_CONV, D_CONV, D_CONV,
          D_ATTN, D_ATTN, D_ATTN, D_ATTN,
          D_MODEL, D_MODEL)
N_IN = sum(SPLITS)

kernel_name = "hybrid_shortconv_stickbreaking_step"


def rms_norm(x, w):
    x32 = x.astype(jnp.float32)
    y = x32 * lax.rsqrt(jnp.mean(x32 * x32, axis=-1, keepdims=True) + EPS)
    return (y * w.astype(jnp.float32)).astype(x.dtype)


def stick_breaking(q, k, v, bias, q_pos, k_pos):
    scale = 1.0 / math.sqrt(q.shape[-1])
    z = jnp.einsum('bqhd,bkhd->bhqk', q.astype(jnp.float32), k.astype(jnp.float32)) * scale
    z = z + bias.astype(jnp.float32)[None, :, None, None]
    mask = k_pos[None, :] < q_pos[:, None]
    log_stay = jnp.where(mask, -jax.nn.softplus(z), 0.0)
    shifted = jnp.concatenate([log_stay[..., 1:], jnp.zeros_like(log_stay[..., :1])], axis=-1)
    rev = lax.cumsum(shifted, axis=log_stay.ndim - 1, reverse=True)
    a = jnp.where(mask, jnp.exp(jax.nn.log_sigmoid(z) + rev), 0.0)
    return jnp.einsum('bhqk,bkhd->bqhd', a.astype(v.dtype), v)


def prompt_attention(q, k, v, bias):
    b, s, h, d = q.shape
    nb = s // Q_BLOCK
    qb = q.reshape(b, nb, Q_BLOCK, h, d).transpose(1, 0, 2, 3, 4)
    pos = jnp.arange(s, dtype=jnp.int32)
    qpos = pos.reshape(nb, Q_BLOCK)
    o = lax.map(lambda a: stick_breaking(a[0], k, v, bias, a[1], pos), (qb, qpos))
    return o.transpose(1, 0, 2, 3, 4).reshape(b, s, h, d)


def sample_attention(q, k_new, v_new, bias, k_past, v_past):
    past = k_past.shape[1]
    t = q.shape[1]
    k_all = jnp.concatenate([k_past, k_new], axis=1)
    v_all = jnp.concatenate([v_past, v_new], axis=1)
    k_pos = jnp.arange(past + t, dtype=jnp.int32)
    q_pos = past + jnp.arange(t, dtype=jnp.int32)
    return stick_breaking(q, k_all, v_all, bias, q_pos, k_pos)


def causal_conv(u, conv_state, conv_w):
    t = u.shape[1]
    up = jnp.concatenate([conv_state.astype(u.dtype), u], axis=1)
    y = conv_w[0] * up[:, 0:t]
    for j in range(1, CONV_W):
        y = y + conv_w[j] * up[:, j:j + t]
    return y, up[:, -(CONV_W - 1):]


def mixer_layer(x, p, conv_state, attend, norm_w, w_in, conv_w, q_norm_w, k_norm_w, attn_bias,
                w_out_conv, w_out_attn, w_o, w_ple, ple_norm_w, w_ple_gate):
    b, t, _ = x.shape
    h = rms_norm(x, norm_w)
    zin = h @ w_in
    idx = list(np.cumsum(SPLITS)[:-1])
    b_c, c_c, x_c, z_c, q, k, v, z_a, g_c, g_a = jnp.split(zin, idx, axis=-1)
    conv_out, new_conv = causal_conv(c_c * x_c, conv_state, conv_w)
    y_c = (b_c * conv_out * jax.nn.silu(z_c)) @ w_out_conv
    q = rms_norm(q.reshape(b, t, N_HEADS, HEAD_DIM), q_norm_w)
    k = rms_norm(k.reshape(b, t, N_HEADS, HEAD_DIM), k_norm_w)
    v = v.reshape(b, t, N_HEADS, HEAD_DIM)
    o = attend(q, k, v, attn_bias).reshape(b, t, D_ATTN)
    y_a = (o * jax.nn.silu(z_a)) @ w_out_attn
    merged = jax.nn.sigmoid(g_c) * y_c + jax.nn.sigmoid(g_a) * y_a
    x1 = x + merged @ w_o
    gate = jax.nn.sigmoid(rms_norm(x1, ple_norm_w) @ w_ple_gate)
    x2 = x1 + (p @ w_ple) * gate
    return x2, k, v, new_conv


def setup_inputs(seed: int = 0) -> dict:
    key = jax.random.key(seed)
    ks = jax.random.split(key, 24)
    n_pages = PAST_LEN // PAGE_SIZE
    n_used = DEC_BATCH * n_pages
    n_pool = (n_used * 5) // 4
    f32 = jnp.float32

    def nrm(k, shape, scale):
        return jax.random.normal(k, shape, f32) * scale

    def gain(k, shape):
        return 1.0 + 0.02 * jax.random.normal(k, shape, f32)

    page_table = jax.random.permutation(ks[0], n_pool)[:n_used].reshape(DEC_BATCH, n_pages).astype(jnp.int32)
    return {
        "x_prompt": nrm(ks[1], (BATCH, SEQ, D_MODEL), 1.0),
        "x_sample": nrm(ks[2], (DEC_BATCH, DEC_SEQ, D_MODEL), 1.0),
        "p_prompt": nrm(ks[3], (DEPTH, BATCH, SEQ, PLE_DIM), 1.0),
        "p_sample": nrm(ks[4], (DEPTH, DEC_BATCH, DEC_SEQ, PLE_DIM), 1.0),
        "cache_k": nrm(ks[5], (DEPTH, n_pool, PAGE_SIZE, N_HEADS, HEAD_DIM), 1.0),
        "cache_v": nrm(ks[6], (DEPTH, n_pool, PAGE_SIZE, N_HEADS, HEAD_DIM), 1.0),
        "state_conv": nrm(ks[7], (DEPTH, DEC_BATCH, CONV_W - 1, D_CONV), 1.0),
        "page_table": page_table,
        "norm_w": gain(ks[8], (DEPTH, D_MODEL)),
        "w_in": nrm(ks[9], (DEPTH, D_MODEL, N_IN), D_MODEL ** -0.5),
        "conv_w": nrm(ks[10], (DEPTH, CONV_W, D_CONV), CONV_W ** -0.5),
        "q_norm_w": gain(ks[11], (DEPTH, HEAD_DIM)),
        "k_norm_w": gain(ks[12], (DEPTH, HEAD_DIM)),
        "attn_bias": SB_BIAS_INIT + 0.1 * jax.random.normal(ks[19], (DEPTH, N_HEADS), f32),
        "w_out_conv": nrm(ks[13], (DEPTH, D_CONV, D_MODEL), D_CONV ** -0.5),
        "w_out_attn": nrm(ks[14], (DEPTH, D_ATTN, D_MODEL), D_ATTN ** -0.5),
        "w_o": nrm(ks[15], (DEPTH, D_MODEL, D_MODEL), D_MODEL ** -0.5),
        "w_ple": nrm(ks[16], (DEPTH, PLE_DIM, D_MODEL), PLE_DIM ** -0.5),
        "ple_norm_w": gain(ks[17], (DEPTH, D_MODEL)),
        "w_ple_gate": nrm(ks[18], (DEPTH, D_MODEL, D_MODEL), D_MODEL ** -0.5),
    }


def reference(x_prompt, x_sample, p_prompt, p_sample, cache_k, cache_v, state_conv, page_table,
              norm_w, w_in, conv_w, q_norm_w, k_norm_w, attn_bias, w_out_conv, w_out_attn, w_o,
              w_ple, ple_norm_w, w_ple_gate):
    n_pages = page_table.shape[1]
    past = n_pages * PAGE_SIZE
    hp = x_prompt
    hs = x_sample
    kp_l, vp_l, cp_l, ks_l, vs_l, cs_l = [], [], [], [], [], []
    for l in range(DEPTH):
        w = (norm_w[l], w_in[l], conv_w[l], q_norm_w[l], k_norm_w[l], attn_bias[l], w_out_conv[l],
             w_out_attn[l], w_o[l], w_ple[l], ple_norm_w[l], w_ple_gate[l])
        zero_state = jnp.zeros((hp.shape[0], CONV_W - 1, D_CONV), hp.dtype)
        hp, kp, vp, cp = mixer_layer(hp, p_prompt[l], zero_state, prompt_attention, *w)
        k_past = cache_k[l][page_table].reshape(hs.shape[0], past, N_HEADS, HEAD_DIM)
        v_past = cache_v[l][page_table].reshape(hs.shape[0], past, N_HEADS, HEAD_DIM)
        attend_s = lambda q, k, v, bb, kp_=k_past, vp_=v_past: sample_attention(q, k, v, bb, kp_, vp_)
        hs, ksm, vsm, csm = mixer_layer(hs, p_sample[l], state_conv[l], attend_s, *w)
        kp_l.append(kp); vp_l.append(vp); cp_l.append(cp)
        ks_l.append(ksm); vs_l.append(vsm); cs_l.append(csm)
    return (hp, hs, jnp.stack(kp_l), jnp.stack(vp_l), jnp.stack(cp_l),
            jnp.stack(ks_l), jnp.stack(vs_l), jnp.stack(cs_l))
```

```python
import functools

import numpy as np
import jax
import jax.numpy as jnp
from jax import lax
from jax.experimental import pallas as pl
from jax.experimental.pallas import tpu as pltpu

F32 = jnp.float32
BF16 = jnp.bfloat16

N_HEADS = 16
HEAD_DIM = 64
EPS = 1e-6
N_SPLITS = 10
KEY_TILE = 128
CONV_TAIL = 8

_NT = (((1,), (1,)), ((), ()))


def _sigmoid(z):
    return 1.0 / (1.0 + jnp.exp(-z))


def _silu(z):
    return z * _sigmoid(z)


def _split_bf16(x):
    hi = x.astype(BF16)
    lo = (x - hi.astype(F32)).astype(BF16)
    return hi, lo


def _suffix_matrix():
    r = np.arange(2 * KEY_TILE)[:, None] % KEY_TILE
    c = np.arange(2 * KEY_TILE)[None, :]
    m = np.where(c < KEY_TILE, (r > c), True)
    return jnp.asarray(m, dtype=BF16)


def _head_sum_matrix(cb):
    r = np.arange(cb)[:, None] // HEAD_DIM
    c = np.arange(cb)[None, :] // HEAD_DIM
    return jnp.asarray(r == c, dtype=BF16)


def _sb_tile(z, mask, c, mm):
    e = jnp.exp(-jnp.abs(z))
    sp = jnp.maximum(z, 0.0) + jnp.log(1.0 + e)
    log_stay = -sp
    log_beta = z - sp
    if mask is not None:
        log_stay = jnp.where(mask, log_stay, 0.0)
    hi, lo = _split_bf16(log_stay)
    rm = jnp.dot(jnp.concatenate([hi, lo], axis=1), mm, preferred_element_type=F32)
    a = jnp.exp(log_beta + rm[:, :KEY_TILE] + c)
    if mask is not None:
        a = jnp.where(mask, a, 0.0)
    return a.astype(BF16), c + rm[:, KEY_TILE:]


def _pre_body(is_prompt, tm, cb, n_cb, tiles_per_seq, *refs):
    x_ref, nw_ref = refs[0:2]
    w_bc, w_cc, w_xc, w_zc, w_q, w_k, w_v, w_za, w_gc, w_ga = refs[2:12]
    cw_ref, qw_ref, kw_ref, woc_ref, g_ref = refs[12:17]
    pos = 17
    if not is_prompt:
        h1_ref, h2_ref = refs[pos:pos + 2]
        pos += 2
    (qb_ref, k_ref, v_ref, sza_ref, sga_ref, ycg_ref, tail_ref) = refs[pos:pos + 7]
    pos += 7
    h_ref, acc_ref, sgc_ref = refs[pos:pos + 3]
    pos += 3
    if is_prompt:
        carry_ref = refs[pos]

    i = pl.program_id(0)
    j = pl.program_id(1)

    @pl.when(j == 0)
    def _():
        x = x_ref[...]
        ms = jnp.mean(x * x, axis=-1, keepdims=True)
        h_ref[...] = (x * lax.rsqrt(ms + EPS) * nw_ref[...]).astype(BF16)

    h = h_ref[...]

    def proj(w_ref):
        return jnp.dot(h, w_ref[...], preferred_element_type=F32)

    u = proj(w_cc) * proj(w_xc)
    row = lax.broadcasted_iota(jnp.int32, (tm, cb), 0)
    r1 = pltpu.roll(u, 1, 0)
    r2 = pltpu.roll(u, 2, 0)
    if is_prompt:
        @pl.when(i % tiles_per_seq == 0)
        def _():
            carry_ref[j] = jnp.zeros((CONV_TAIL, cb), F32)

        cr = carry_ref[j]
        c6 = cr[CONV_TAIL - 2:CONV_TAIL - 1, :]
        c7 = cr[CONV_TAIL - 1:CONV_TAIL, :]
        u1 = jnp.where(row >= 1, r1, c7)
        u2 = jnp.where(row >= 2, r2, jnp.where(row == 0, c6, c7))
        tail = u[tm - CONV_TAIL:, :]
        carry_ref[j] = tail
        tail_ref[0] = tail
    else:
        t = row & 7
        u1 = jnp.where(t >= 1, r1, h1_ref[...])
        u2 = jnp.where(t >= 2, r2, h2_ref[...])
        tail_ref[...] = u
    conv = cw_ref[2:3, :] * u + cw_ref[1:2, :] * u1 + cw_ref[0:1, :] * u2

    g = (proj(w_bc) * conv * _silu(proj(w_zc))).astype(BF16)
    part = jnp.dot(g, woc_ref[...], preferred_element_type=F32)

    @pl.when(j == 0)
    def _():
        acc_ref[...] = part

    @pl.when(j > 0)
    def _():
        acc_ref[...] += part

    col = pl.ds(pl.multiple_of(j * cb, cb), cb)
    sgc_ref[:, col] = _sigmoid(proj(w_gc))

    @pl.when(j == n_cb - 1)
    def _():
        ycg_ref[...] = sgc_ref[...] * acc_ref[...]

    gm = g_ref[...]

    def head_norm(v, w_row):
        hi, lo = _split_bf16(v * v)
        ss = (jnp.dot(hi, gm, preferred_element_type=F32)
              + jnp.dot(lo, gm, preferred_element_type=F32))
        return v * lax.rsqrt(ss * (1.0 / HEAD_DIM) + EPS) * w_row

    qn = head_norm(proj(w_q), qw_ref[...])
    qb_ref[...] = (qn * (HEAD_DIM ** -0.5)).astype(BF16)
    kn = head_norm(proj(w_k), kw_ref[...])
    v = proj(w_v)
    if is_prompt:
        k_ref[0] = kn.T
        v_ref[0] = v.T
    else:
        k_ref[...] = kn
        v_ref[...] = v
    sza_ref[...] = _silu(proj(w_za))
    sga_ref[...] = _sigmoid(proj(w_ga))


def _pre_call(x, norm_w, w_in_bf, conv_w, qw_t, kw_t, woc_bf, seq_len, hist=None):
    n, d = x.shape
    is_prompt = hist is None
    cb = 256
    n_cb = d // cb
    tm = 512 if is_prompt else n
    n_tiles = n // tm
    tiles_per_seq = seq_len // tm if is_prompt else 1
    tail_rows = CONV_TAIL if is_prompt else tm

    tok = lambda i, j: (i, j)
    in_specs = [
        pl.BlockSpec((tm, d), lambda i, j: (i, 0)),
        pl.BlockSpec((1, d), lambda i, j: (0, 0)),
    ]
    for s in range(N_SPLITS):
        in_specs.append(pl.BlockSpec((d, cb), functools.partial(lambda i, j, s: (0, s * n_cb + j), s=s)))
    in_specs += [
        pl.BlockSpec((3, cb), lambda i, j: (0, j)),
        pl.BlockSpec((1, cb), lambda i, j: (0, 0)),
        pl.BlockSpec((1, cb), lambda i, j: (0, 0)),
        pl.BlockSpec((cb, d), lambda i, j: (j, 0)),
        pl.BlockSpec((cb, cb), lambda i, j: (0, 0)),
    ]
    args = [x, norm_w, *([w_in_bf] * N_SPLITS), conv_w, qw_t, kw_t, woc_bf, _head_sum_matrix(cb)]
    if not is_prompt:
        in_specs += [pl.BlockSpec((tm, cb), tok), pl.BlockSpec((tm, cb), tok)]
        args += list(hist)

    if is_prompt:
        kv_shape = jax.ShapeDtypeStruct((n // seq_len, d, seq_len), F32)
        kv_spec = pl.BlockSpec((1, cb, tm), lambda i, j: (i // tiles_per_seq, j, i % tiles_per_seq))
    else:
        kv_shape = jax.ShapeDtypeStruct((n, d), F32)
        kv_spec = pl.BlockSpec((tm, cb), tok)
    out_shape = [
        jax.ShapeDtypeStruct((n, d), BF16), kv_shape, kv_shape,
        jax.ShapeDtypeStruct((n, d), F32), jax.ShapeDtypeStruct((n, d), F32),
        jax.ShapeDtypeStruct((n, d), F32),
    ]
    out_specs = ([pl.BlockSpec((tm, cb), tok), kv_spec, kv_spec]
                 + [pl.BlockSpec((tm, cb), tok)] * 2 + [pl.BlockSpec((tm, d), lambda i, j: (i, 0))])
    if is_prompt:
        out_shape.append(jax.ShapeDtypeStruct((n_tiles, CONV_TAIL, d), F32))
        out_specs.append(pl.BlockSpec((1, CONV_TAIL, cb), lambda i, j: (i, 0, j)))
    else:
        out_shape.append(jax.ShapeDtypeStruct((n, d), F32))
        out_specs.append(pl.BlockSpec((tm, cb), tok))

    scratch = [pltpu.VMEM((tm, d), BF16), pltpu.VMEM((tm, d), F32), pltpu.VMEM((tm, d), F32)]
    if is_prompt:
        scratch.append(pltpu.VMEM((n_cb, CONV_TAIL, cb), F32))

    return pl.pallas_call(
        functools.partial(_pre_body, is_prompt, tm, cb, n_cb, tiles_per_seq),
        grid=(n_tiles, n_cb),
        in_specs=in_specs,
        out_specs=out_specs,
        out_shape=out_shape,
        scratch_shapes=scratch,
        compiler_params=pltpu.CompilerParams(
            dimension_semantics=("arbitrary", "arbitrary"),
            vmem_limit_bytes=56 * 1024 * 1024),
        name="pre_prompt" if is_prompt else "pre_sample",
    )(*args)


def _pattn_body(nt, q_ref, k_ref, v_ref, b_ref, mm_ref, o_ref, kk_ref, vv_ref):
    t = nt * KEY_TILE
    chan = lax.broadcasted_iota(jnp.int32, (2 * HEAD_DIM, t), 0)
    first = chan < HEAD_DIM
    k2 = k_ref[0]
    v2 = v_ref[0]
    k_h0 = jnp.where(first, k2, 0.0).astype(BF16)
    k_h1 = jnp.where(first, 0.0, k2).astype(BF16)
    v_h0 = jnp.where(first, v2, 0.0).astype(BF16)
    v_h1 = jnp.where(first, 0.0, v2).astype(BF16)
    for j in range(nt):
        cols = slice(j * KEY_TILE, (j + 1) * KEY_TILE)
        kk_ref[j, :, 0:KEY_TILE] = k_h0[:, cols]
        kk_ref[j, :, KEY_TILE:] = k_h1[:, cols]
        vv_ref[j, :, 0:KEY_TILE] = v_h0[:, cols]
        vv_ref[j, :, KEY_TILE:] = v_h1[:, cols]

    bias = b_ref[0, 0:1, :]
    mm = mm_ref[...]
    r = lax.broadcasted_iota(jnp.int32, (2 * KEY_TILE, KEY_TILE), 0) & (KEY_TILE - 1)
    cidx = lax.broadcasted_iota(jnp.int32, (2 * KEY_TILE, KEY_TILE), 1)
    diag_mask = cidx < r

    def tile(q_i, j, mask, c):
        z2 = jnp.dot(q_i, kk_ref[j], preferred_element_type=F32) + bias
        z = jnp.concatenate([z2[:, :KEY_TILE], z2[:, KEY_TILE:]], axis=0)
        a, c = _sb_tile(z, mask, c, mm)
        a2 = jnp.concatenate([a[:KEY_TILE], a[KEY_TILE:]], axis=1)
        return lax.dot_general(a2, vv_ref[j], _NT, preferred_element_type=F32), c

    def q_loop(i, carry):
        r0 = pl.multiple_of(i * KEY_TILE, KEY_TILE)
        q_i = q_ref[0, pl.ds(r0, KEY_TILE), :]
        o, c = tile(q_i, i, diag_mask, jnp.zeros((2 * KEY_TILE, KEY_TILE), F32))

        def k_loop(jj, oc):
            o2, c2 = tile(q_i, i - 1 - jj, None, oc[1])
            return oc[0] + o2, c2

        o, c = lax.fori_loop(0, i, k_loop, (o, c))
        o_ref[0, pl.ds(r0, KEY_TILE), :] = o
        return carry

    lax.fori_loop(0, nt, q_loop, 0)


def _pattn_call(qb, kt, vt, bias_pairs, batch, seq_len):
    d = qb.shape[-1]
    n_hp = d // (2 * HEAD_DIM)
    nt = seq_len // KEY_TILE
    blk = pl.BlockSpec((1, seq_len, 2 * HEAD_DIM), lambda b, hp: (b, 0, hp))
    blk_t = pl.BlockSpec((1, 2 * HEAD_DIM, seq_len), lambda b, hp: (b, hp, 0))
    return pl.pallas_call(
        functools.partial(_pattn_body, nt),
        grid=(batch, n_hp),
        in_specs=[blk, blk_t, blk_t,
                  pl.BlockSpec((1, 8, 4 * HEAD_DIM), lambda b, hp: (hp, 0, 0)),
                  pl.BlockSpec((2 * KEY_TILE, 2 * KEY_TILE), lambda b, hp: (0, 0))],
        out_specs=blk,
        out_shape=jax.ShapeDtypeStruct((batch, seq_len, d), F32),
        scratch_shapes=[pltpu.VMEM((nt, 2 * HEAD_DIM, 2 * KEY_TILE), BF16),
                        pltpu.VMEM((nt, 2 * HEAD_DIM, 2 * KEY_TILE), BF16)],
        compiler_params=pltpu.CompilerParams(
            dimension_semantics=("arbitrary", "arbitrary"),
            vmem_limit_bytes=32 * 1024 * 1024),
        name="prompt_attn",
    )(qb.reshape(batch, seq_len, d), kt, vt, bias_pairs, _suffix_matrix())


def _sattn_body(pg, n_steps, t_new, pt_ref, q_ref, kn_ref, vn_ref, b_ref, mm_ref, *rest):
    k_refs = rest[:pg]
    v_refs = rest[pg:2 * pg]
    o_ref = rest[2 * pg]
    qr_ref, kb_ref, vb_ref, c_ref, acc_ref = rest[2 * pg + 1:]
    d = q_ref.shape[-1]
    rows = N_HEADS * t_new
    s = pl.program_id(1)
    mm = mm_ref[...]
    bias = b_ref[...]
    row = lax.broadcasted_iota(jnp.int32, (rows, d), 0)
    lane = lax.broadcasted_iota(jnp.int32, (rows, d), 1)
    own_head = (row // t_new) == (lane // HEAD_DIM)

    @pl.when(s == 0)
    def _():
        qt = jnp.concatenate([q_ref[...]] * N_HEADS, axis=0)
        qr = jnp.where(own_head, qt, 0.0).astype(BF16)
        qr_ref[...] = qr
        pad = jnp.zeros((KEY_TILE - t_new, d), F32)
        kn = jnp.concatenate([kn_ref[...], pad], axis=0).astype(BF16)
        vn = jnp.concatenate([vn_ref[...], pad], axis=0).astype(BF16)
        z = lax.dot_general(qr, kn, _NT, preferred_element_type=F32) + bias
        rr = lax.broadcasted_iota(jnp.int32, (rows, KEY_TILE), 0)
        ll = lax.broadcasted_iota(jnp.int32, (rows, KEY_TILE), 1)
        mask = ll < (rr % t_new)
        a, c = _sb_tile(z, mask, jnp.zeros((rows, KEY_TILE), F32), mm)
        c_ref[...] = c
        acc_ref[...] = jnp.dot(a, vn, preferred_element_type=F32)

    for p in range(pg):
        kb_ref[:, p * KEY_TILE:(p + 1) * KEY_TILE] = k_refs[p][0].reshape(d, KEY_TILE).astype(BF16)
        vb_ref[:, p * KEY_TILE:(p + 1) * KEY_TILE] = v_refs[p][0].reshape(d, KEY_TILE).astype(BF16)
    z = jnp.dot(qr_ref[...], kb_ref[...], preferred_element_type=F32)
    c = c_ref[...]
    a_tiles = [None] * pg
    for p in reversed(range(pg)):
        a_tiles[p], c = _sb_tile(z[:, p * KEY_TILE:(p + 1) * KEY_TILE] + bias, None, c, mm)
    c_ref[...] = c
    acc_ref[...] += lax.dot_general(jnp.concatenate(a_tiles, axis=1), vb_ref[...], _NT,
                                    preferred_element_type=F32)

    @pl.when(s == n_steps - 1)
    def _():
        m = jnp.where(own_head, acc_ref[...], 0.0)
        out = m[0:t_new, :]
        for hh in range(1, N_HEADS):
            out = out + m[hh * t_new:(hh + 1) * t_new, :]
        o_ref[...] = out


def _sattn_call(q, k_new, v_new, bias_rows, cache_k, cache_v, page_table, t_new):
    n, d = q.shape
    batch = n // t_new
    n_pages = page_table.shape[1]
    pg = 8
    n_steps = n_pages // pg
    rows = N_HEADS * t_new

    tok = pl.BlockSpec((t_new, d), lambda b, s, pt: (b, 0))

    def page_spec(p):
        return pl.BlockSpec((1, N_HEADS, HEAD_DIM, KEY_TILE),
                            lambda b, s, pt, p=p: (pt[b, (n_steps - 1 - s) * pg + p], 0, 0, 0))

    grid_spec = pltpu.PrefetchScalarGridSpec(
        num_scalar_prefetch=1,
        grid=(batch, n_steps),
        in_specs=[tok, tok, tok,
                  pl.BlockSpec((rows, KEY_TILE), lambda b, s, pt: (0, 0)),
                  pl.BlockSpec((2 * KEY_TILE, 2 * KEY_TILE), lambda b, s, pt: (0, 0))]
                 + [page_spec(p) for p in range(pg)] * 2,
        out_specs=tok,
        scratch_shapes=[pltpu.VMEM((rows, d), BF16),
                        pltpu.VMEM((d, pg * KEY_TILE), BF16),
                        pltpu.VMEM((d, pg * KEY_TILE), BF16),
                        pltpu.VMEM((rows, KEY_TILE), F32),
                        pltpu.VMEM((rows, d), F32)],
    )
    return pl.pallas_call(
        functools.partial(_sattn_body, pg, n_steps, t_new),
        grid_spec=grid_spec,
        out_shape=jax.ShapeDtypeStruct((n, d), F32),
        compiler_params=pltpu.CompilerParams(
            dimension_semantics=("arbitrary", "arbitrary"),
            vmem_limit_bytes=48 * 1024 * 1024),
        name="sample_attn",
    )(page_table, q, k_new, v_new, bias_rows, _suffix_matrix(),
      *([cache_k] * pg), *([cache_v] * pg))


def _post_body(o_ref, sza_ref, sga_ref, ycg_ref, x_ref, p_ref, woa_ref, wo_ref, wpg_ref, wpl_ref,
               pnw_ref, y_ref):
    ga = (o_ref[...] * sza_ref[...]).astype(BF16)
    y_a = jnp.dot(ga, woa_ref[...], preferred_element_type=F32)
    merged = ycg_ref[...] + sga_ref[...] * y_a
    x1 = x_ref[...] + jnp.dot(merged.astype(BF16), wo_ref[...], preferred_element_type=F32)
    ms = jnp.mean(x1 * x1, axis=-1, keepdims=True)
    nrm = (x1 * lax.rsqrt(ms + EPS) * pnw_ref[...]).astype(BF16)
    gate = _sigmoid(jnp.dot(nrm, wpg_ref[...], preferred_element_type=F32))
    ple = jnp.dot(p_ref[...].astype(BF16), wpl_ref[...], preferred_element_type=F32)
    y_ref[...] = x1 + ple * gate


def _post_call(o, sza, sga, ycg, x, p, woa_bf, wo_bf, wpg_bf, wpl_bf, pnw):
    n, d = x.shape
    pd = p.shape[-1]
    tm = min(512, n)
    tokd = pl.BlockSpec((tm, d), lambda i: (i, 0))
    full = lambda shape: pl.BlockSpec(shape, lambda i: (0, 0))
    return pl.pallas_call(
        _post_body,
        grid=(n // tm,),
        in_specs=[tokd, tokd, tokd, tokd, tokd, pl.BlockSpec((tm, pd), lambda i: (i, 0)),
                  full((d, d)), full((d, d)), full((d, d)), full((pd, d)), full((1, d))],
        out_specs=tokd,
        out_shape=jax.ShapeDtypeStruct((n, d), F32),
        compiler_params=pltpu.CompilerParams(
            dimension_semantics=("arbitrary",),
            vmem_limit_bytes=56 * 1024 * 1024),
        name="post",
    )(o, sza, sga, ycg, x, p, woa_bf, wo_bf, wpg_bf, wpl_bf, pnw)


def kernel(x_prompt, x_sample, p_prompt, p_sample, cache_k, cache_v, state_conv, page_table, norm_w, w_in, conv_w, q_norm_w, k_norm_w, attn_bias, w_out_conv, w_out_attn, w_o, w_ple, ple_norm_w, w_ple_gate):
    depth = norm_w.shape[0]
    batch, seq_len, d = x_prompt.shape
    dec_batch, dec_seq, _ = x_sample.shape
    n_pool, page = cache_k.shape[1], cache_k.shape[2]
    assert page == KEY_TILE and d == N_HEADS * HEAD_DIM and dec_seq == CONV_TAIL
    n_p = batch * seq_len
    n_s = dec_batch * dec_seq
    cb = 256

    hp = x_prompt.reshape(n_p, d)
    hs = x_sample.reshape(n_s, d)
    outs = [[] for _ in range(6)]
    for l in range(depth):
        w_in_bf = w_in[l].astype(BF16)
        woc_bf = w_out_conv[l].astype(BF16)
        woa_bf = w_out_attn[l].astype(BF16)
        wo_bf = w_o[l].astype(BF16)
        wpg_bf = w_ple_gate[l].astype(BF16)
        wpl_bf = w_ple[l].astype(BF16)
        nw = norm_w[l].reshape(1, d)
        pnw = ple_norm_w[l].reshape(1, d)
        qw_t = jnp.tile(q_norm_w[l], cb // HEAD_DIM).reshape(1, cb)
        kw_t = jnp.tile(k_norm_w[l], cb // HEAD_DIM).reshape(1, cb)
        bias = attn_bias[l]
        bias_pairs = jnp.broadcast_to(
            jnp.repeat(bias.reshape(N_HEADS // 2, 2), KEY_TILE, axis=1)[:, None, :],
            (N_HEADS // 2, 8, 2 * KEY_TILE))
        bias_rows = jnp.broadcast_to(jnp.repeat(bias, dec_seq)[:, None], (N_HEADS * dec_seq, KEY_TILE))

        qb, kt, vt, sza, sga, ycg, tails = _pre_call(
            hp, nw, w_in_bf, conv_w[l], qw_t, kw_t, woc_bf, seq_len)
        o = _pattn_call(qb, kt, vt, bias_pairs, batch, seq_len).reshape(n_p, d)
        hp = _post_call(o, sza, sga, ycg, hp, p_prompt[l].reshape(n_p, -1),
                        woa_bf, wo_bf, wpg_bf, wpl_bf, pnw)
        to_bthd = lambda a: a.reshape(batch, N_HEADS, HEAD_DIM, seq_len).transpose(0, 3, 1, 2)
        outs[0].append(to_bthd(kt))
        outs[1].append(to_bthd(vt))
        outs[2].append(tails.reshape(batch, -1, CONV_TAIL, d)[:, -1, CONV_TAIL - 2:, :])

        st = state_conv[l]
        zrow = jnp.zeros((dec_batch, dec_seq, d), F32)
        hist1 = jnp.concatenate([st[:, 1:2], zrow[:, 1:]], axis=1).reshape(n_s, d)
        hist2 = jnp.concatenate([st[:, 0:2], zrow[:, 2:]], axis=1).reshape(n_s, d)
        qb, kf, vf, sza, sga, ycg, u_s = _pre_call(
            hs, nw, w_in_bf, conv_w[l], qw_t, kw_t, woc_bf, dec_seq, hist=(hist1, hist2))
        o = _sattn_call(qb.astype(F32), kf, vf, bias_rows,
                        cache_k[l].transpose(0, 2, 3, 1), cache_v[l].transpose(0, 2, 3, 1),
                        page_table, dec_seq)
        hs = _post_call(o, sza, sga, ycg, hs, p_sample[l].reshape(n_s, -1),
                        woa_bf, wo_bf, wpg_bf, wpl_bf, pnw)
        outs[3].append(kf.reshape(dec_batch, dec_seq, N_HEADS, HEAD_DIM))
        outs[4].append(vf.reshape(dec_batch, dec_seq, N_HEADS, HEAD_DIM))
        outs[5].append(u_s.reshape(dec_batch, dec_seq, d)[:, dec_seq - 2:, :])

    return (hp.reshape(batch, seq_len, d), hs.reshape(dec_batch, dec_seq, d),
            jnp.stack(outs[0]), jnp.stack(outs[1]), jnp.stack(outs[2]),
            jnp.stack(outs[3]), jnp.stack(outs[4]), jnp.stack(outs[5]))
```

```python
import functools

import numpy as np
import jax
import jax.numpy as jnp
from jax import lax
from jax.experimental import pallas as pl
from jax.experimental.pallas import tpu as pltpu

F32 = jnp.float32
BF16 = jnp.bfloat16

N_HEADS = 16
HEAD_DIM = 64
EPS = 1e-6
LOG2E = 1.4426950408889634
N_SPLITS = 10
KEY_TILE = 128
PAIR = 2 * KEY_TILE
CONV_TAIL = 8
Q_CHUNK = 512
PAIRS_PER_STEP = 2

_NT = (((1,), (1,)), ((), ()))


def _sigmoid(z):
    return 1.0 / (1.0 + jnp.exp(-z))


def _silu(z):
    return z * _sigmoid(z)


def _pair_suffix_matrix():
    k = np.arange(PAIR)[:, None]
    c = np.arange(2 * PAIR)[None, :]
    m = np.where(c < PAIR, k > c, c < PAIR + KEY_TILE)
    return jnp.asarray(-m.astype(np.float32), dtype=BF16)


def _head_sum_matrix(cb):
    r = np.arange(cb)[:, None] // HEAD_DIM
    c = np.arange(cb)[None, :] // HEAD_DIM
    return jnp.asarray(r == c, dtype=BF16)


def _neg_abs(z):
    bits = lax.bitcast_convert_type(z, jnp.uint32) | jnp.uint32(0x80000000)
    return lax.bitcast_convert_type(bits, F32)


def _sb_pair(z, mask, c, wm):
    e = jnp.exp2(_neg_abs(z))
    sp = jnp.maximum(z, 0.0) + jnp.log(1.0 + e) * LOG2E
    log_beta = z - sp
    if mask is not None:
        sp = jnp.where(mask, sp, 0.0)
    out = jnp.dot(sp.astype(BF16), wm, preferred_element_type=F32)
    a = jnp.exp2(log_beta + out[:, :PAIR] + jnp.concatenate([c, c], axis=1))
    if mask is not None:
        a = jnp.where(mask, a, 0.0)
    return a.astype(BF16), c + out[:, PAIR:PAIR + KEY_TILE]


def _pre_body(is_prompt, tm, cb, tiles_per_seq, *refs):
    x_ref, nw_ref = refs[0:2]
    w_bc, w_cc, w_xc, w_zc, w_q, w_k, w_v, w_za, w_gc, w_ga = refs[2:12]
    cw_ref, qw_ref, kw_ref, g_ref = refs[12:16]
    pos = 16
    if not is_prompt:
        h1_ref, h2_ref = refs[pos:pos + 2]
        pos += 2
    (qb_ref, k_ref, v_ref, gc_ref, sgc_ref, sza_ref, sga_ref, tail_ref) = refs[pos:pos + 8]
    pos += 8
    if is_prompt:
        carry_ref = refs[pos]

    i = pl.program_id(1)

    x = x_ref[...]
    ms = jnp.mean(x * x, axis=-1, keepdims=True)
    h = (x * lax.rsqrt(ms + EPS) * nw_ref[...]).astype(BF16)

    def proj(w_ref):
        return jnp.dot(h, w_ref[...], preferred_element_type=F32)

    u = proj(w_cc) * proj(w_xc)
    row = lax.broadcasted_iota(jnp.int32, (tm, cb), 0)
    r1 = pltpu.roll(u, 1, 0)
    r2 = pltpu.roll(u, 2, 0)
    if is_prompt:
        @pl.when(i % tiles_per_seq == 0)
        def _():
            carry_ref[...] = jnp.zeros((CONV_TAIL, cb), F32)

        cr = carry_ref[...]
        c6 = cr[CONV_TAIL - 2:CONV_TAIL - 1, :]
        c7 = cr[CONV_TAIL - 1:CONV_TAIL, :]
        u1 = jnp.where(row >= 1, r1, c7)
        u2 = jnp.where(row >= 2, r2, jnp.where(row == 0, c6, c7))
        tail = u[tm - CONV_TAIL:, :]
        carry_ref[...] = tail
        tail_ref[0] = tail
    else:
        t = row & 7
        u1 = jnp.where(t >= 1, r1, h1_ref[...])
        u2 = jnp.where(t >= 2, r2, h2_ref[...])
        tail_ref[...] = u
    conv = cw_ref[2:3, :] * u + cw_ref[1:2, :] * u1 + cw_ref[0:1, :] * u2
    gc_ref[...] = (proj(w_bc) * conv * _silu(proj(w_zc))).astype(BF16)
    sgc_ref[...] = _sigmoid(proj(w_gc))

    gm = g_ref[...]

    def head_norm(v, w_row):
        ss = jnp.dot((v * v).astype(BF16), gm, preferred_element_type=F32)
        return v * lax.rsqrt(ss * (1.0 / HEAD_DIM) + EPS) * w_row

    qn = head_norm(proj(w_q), qw_ref[...])
    qb_ref[...] = (qn * (HEAD_DIM ** -0.5 * LOG2E)).astype(BF16)
    kn = head_norm(proj(w_k), kw_ref[...])
    v = proj(w_v)
    if is_prompt:
        k_ref[0] = kn.T
        v_ref[0] = v.T
    else:
        k_ref[...] = kn
        v_ref[...] = v
    sza_ref[...] = _silu(proj(w_za))
    sga_ref[...] = _sigmoid(proj(w_ga))


def _pre_call(x, norm_w, w_in_bf, conv_w, qw_t, kw_t, cb, seq_len, hist=None):
    n, d = x.shape
    is_prompt = hist is None
    n_cb = d // cb
    tm = 512 if is_prompt else n
    n_tiles = n // tm
    tiles_per_seq = seq_len // tm if is_prompt else 1

    tok = lambda j, i: (i, j)
    in_specs = [
        pl.BlockSpec((tm, d), lambda j, i: (i, 0)),
        pl.BlockSpec((1, d), lambda j, i: (0, 0)),
    ]
    for s in range(N_SPLITS):
        in_specs.append(pl.BlockSpec((d, cb), functools.partial(lambda j, i, s: (0, s * n_cb + j), s=s)))
    in_specs += [
        pl.BlockSpec((3, cb), lambda j, i: (0, j)),
        pl.BlockSpec((1, cb), lambda j, i: (0, 0)),
        pl.BlockSpec((1, cb), lambda j, i: (0, 0)),
        pl.BlockSpec((cb, cb), lambda j, i: (0, 0)),
    ]
    args = [x, norm_w, *([w_in_bf] * N_SPLITS), conv_w, qw_t, kw_t, _head_sum_matrix(cb)]
    if not is_prompt:
        in_specs += [pl.BlockSpec((tm, cb), tok), pl.BlockSpec((tm, cb), tok)]
        args += list(hist)

    if is_prompt:
        kv_shape = jax.ShapeDtypeStruct((n // seq_len, d, seq_len), F32)
        kv_spec = pl.BlockSpec((1, cb, tm), lambda j, i: (i // tiles_per_seq, j, i % tiles_per_seq))
        tail_shape = jax.ShapeDtypeStruct((n_tiles, CONV_TAIL, d), F32)
        tail_spec = pl.BlockSpec((1, CONV_TAIL, cb), lambda j, i: (i, 0, j))
    else:
        kv_shape = jax.ShapeDtypeStruct((n, d), F32)
        kv_spec = pl.BlockSpec((tm, cb), tok)
        tail_shape = jax.ShapeDtypeStruct((n, d), F32)
        tail_spec = pl.BlockSpec((tm, cb), tok)
    tok_spec = pl.BlockSpec((tm, cb), tok)
    out_shape = [jax.ShapeDtypeStruct((n, d), BF16), kv_shape, kv_shape,
                 jax.ShapeDtypeStruct((n, d), BF16),
                 jax.ShapeDtypeStruct((n, d), F32), jax.ShapeDtypeStruct((n, d), F32),
                 jax.ShapeDtypeStruct((n, d), F32), tail_shape]
    out_specs = [tok_spec, kv_spec, kv_spec, tok_spec, tok_spec, tok_spec, tok_spec, tail_spec]

    return pl.pallas_call(
        functools.partial(_pre_body, is_prompt, tm, cb, tiles_per_seq),
        grid=(n_cb, n_tiles),
        in_specs=in_specs,
        out_specs=out_specs,
        out_shape=out_shape,
        scratch_shapes=[pltpu.VMEM((CONV_TAIL, cb), F32)] if is_prompt else [],
        compiler_params=pltpu.CompilerParams(
            dimension_semantics=("arbitrary", "arbitrary"),
            vmem_limit_bytes=56 * 1024 * 1024),
        name="pre_prompt" if is_prompt else "pre_sample",
    )(*args)


def _pattn_body(nt, q_ref, k_ref, v_ref, brow_ref, wm_ref, o_ref, kk_ref, vv_ref, c_ref):
    t = nt * KEY_TILE
    chan = lax.broadcasted_iota(jnp.int32, (2 * HEAD_DIM, t), 0)
    first = chan < HEAD_DIM
    k2 = k_ref[0]
    v2 = v_ref[0]
    k_h0 = jnp.where(first, k2, 0.0).astype(BF16)
    k_h1 = jnp.where(first, 0.0, k2).astype(BF16)
    v_h0 = jnp.where(first, v2, 0.0).astype(BF16)
    v_h1 = jnp.where(first, 0.0, v2).astype(BF16)
    b1 = brow_ref[0]
    brow = jnp.concatenate([b1[:, :KEY_TILE], b1[:, :KEY_TILE], b1[:, KEY_TILE:], b1[:, KEY_TILE:]],
                           axis=1)
    for p in range(nt // 2):
        cols = slice(p * PAIR, (p + 1) * PAIR)
        kk_ref[p, 0:KEY_TILE, 0:PAIR] = k_h0[:, cols]
        kk_ref[p, 0:KEY_TILE, PAIR:] = k_h1[:, cols]
        kk_ref[p, KEY_TILE:, :] = brow
        vv_ref[p, :, 0:PAIR] = v_h0[:, cols]
        vv_ref[p, :, PAIR:] = v_h1[:, cols]

    wm = wm_ref[...]
    ones = jnp.ones((Q_CHUNK, KEY_TILE), BF16)

    def pair(q_rows, p, mask, c):
        m = q_rows.shape[0]
        z4 = jnp.dot(q_rows, kk_ref[p], preferred_element_type=F32)
        z = jnp.concatenate([z4[:, :PAIR], z4[:, PAIR:]], axis=0)
        a, c = _sb_pair(z, mask, c, wm)
        a4 = jnp.concatenate([a[:m], a[m:]], axis=1)
        return lax.dot_general(a4, vv_ref[p], _NT, preferred_element_type=F32), c

    def band_mask(m):
        rr = lax.broadcasted_iota(jnp.int32, (m, PAIR), 0)
        cc = lax.broadcasted_iota(jnp.int32, (m, PAIR), 1)
        mk = cc < rr
        return jnp.concatenate([mk, mk], axis=0)

    half = Q_CHUNK // 2
    mask_upper = band_mask(half)
    mask_all = band_mask(Q_CHUNK)

    for ic in range(t // Q_CHUNK):
        r0 = ic * Q_CHUNK
        q_c = jnp.concatenate([q_ref[0, r0:r0 + Q_CHUNK, :], ones], axis=1)
        zc = jnp.zeros((Q_CHUNK, KEY_TILE), F32)
        o_up, c_up = pair(q_c[half:], 2 * ic + 1, mask_upper, zc)
        c0 = jnp.concatenate([zc[:half], c_up[:half], zc[:half], c_up[half:]], axis=0)
        o_c, cst = pair(q_c, 2 * ic, mask_all, c0)
        o_c = o_c + jnp.concatenate([jnp.zeros((half, 2 * HEAD_DIM), F32), o_up], axis=0)
        o_ref[0, r0:r0 + Q_CHUNK, :] = o_c
        if ic > 0:
            c_ref[...] = cst

            def k_loop(g, carry):
                cst = c_ref[...]
                o_acc = o_ref[0, r0:r0 + Q_CHUNK, :]
                for u in range(PAIRS_PER_STEP):
                    o_new, cst = pair(q_c, 2 * ic - 1 - (g * PAIRS_PER_STEP + u), None, cst)
                    o_acc = o_acc + o_new
                c_ref[...] = cst
                o_ref[0, r0:r0 + Q_CHUNK, :] = o_acc
                return carry

            lax.fori_loop(0, 2 * ic // PAIRS_PER_STEP, k_loop, 0)


def _pattn_call(qb, kt, vt, bias_rows, batch, seq_len):
    d = qb.shape[-1]
    n_hp = d // (2 * HEAD_DIM)
    nt = seq_len // KEY_TILE
    assert seq_len % Q_CHUNK == 0 and Q_CHUNK == 2 * PAIR and PAIRS_PER_STEP == 2
    blk = pl.BlockSpec((1, seq_len, 2 * HEAD_DIM), lambda b, hp: (b, 0, hp))
    blk_t = pl.BlockSpec((1, 2 * HEAD_DIM, seq_len), lambda b, hp: (b, hp, 0))
    return pl.pallas_call(
        functools.partial(_pattn_body, nt),
        grid=(batch, n_hp),
        in_specs=[blk, blk_t, blk_t,
                  pl.BlockSpec((1, KEY_TILE, PAIR), lambda b, hp: (hp, 0, 0)),
                  pl.BlockSpec((PAIR, 2 * PAIR), lambda b, hp: (0, 0))],
        out_specs=blk,
        out_shape=jax.ShapeDtypeStruct((batch, seq_len, d), F32),
        scratch_shapes=[pltpu.VMEM((nt // 2, PAIR, 2 * PAIR), BF16),
                        pltpu.VMEM((nt // 2, 2 * HEAD_DIM, 2 * PAIR), BF16),
                        pltpu.VMEM((2 * Q_CHUNK, KEY_TILE), F32)],
        compiler_params=pltpu.CompilerParams(
            dimension_semantics=("arbitrary", "arbitrary"),
            vmem_limit_bytes=48 * 1024 * 1024),
        name="prompt_attn",
    )(qb.reshape(batch, seq_len, d), kt, vt, bias_rows, _pair_suffix_matrix())


def _sattn_body(pg, n_steps, t_new, pt_ref, q_ref, kn_ref, vn_ref, b_ref, wm_ref, *rest):
    k_refs = rest[:pg]
    v_refs = rest[pg:2 * pg]
    o_ref = rest[2 * pg]
    qr_ref, kb_ref, vb_ref, c_ref, acc_ref = rest[2 * pg + 1:]
    d = q_ref.shape[-1]
    rows = N_HEADS * t_new
    s = pl.program_id(1)
    wm = wm_ref[...]
    bias = b_ref[...]
    row = lax.broadcasted_iota(jnp.int32, (rows, d), 0)
    lane = lax.broadcasted_iota(jnp.int32, (rows, d), 1)
    own_head = (row // t_new) == (lane // HEAD_DIM)

    @pl.when(s == 0)
    def _():
        qt = jnp.concatenate([q_ref[...]] * N_HEADS, axis=0)
        qr = jnp.where(own_head, qt, 0.0).astype(BF16)
        qr_ref[...] = qr
        pad = jnp.zeros((PAIR - t_new, d), F32)
        kn = jnp.concatenate([kn_ref[...], pad], axis=0).astype(BF16)
        vn = jnp.concatenate([vn_ref[...], pad], axis=0).astype(BF16)
        z = lax.dot_general(qr, kn, _NT, preferred_element_type=F32) + bias
        rr = lax.broadcasted_iota(jnp.int32, (rows, PAIR), 0)
        ll = lax.broadcasted_iota(jnp.int32, (rows, PAIR), 1)
        mask = ll < (rr % t_new)
        a, c = _sb_pair(z, mask, jnp.zeros((rows, KEY_TILE), F32), wm)
        c_ref[...] = c
        acc_ref[...] = jnp.dot(a, vn, preferred_element_type=F32)

    for p in range(pg):
        kb_ref[:, p * KEY_TILE:(p + 1) * KEY_TILE] = k_refs[p][0].reshape(d, KEY_TILE).astype(BF16)
        vb_ref[:, p * KEY_TILE:(p + 1) * KEY_TILE] = v_refs[p][0].reshape(d, KEY_TILE).astype(BF16)
    z = jnp.dot(qr_ref[...], kb_ref[...], preferred_element_type=F32)
    c = c_ref[...]
    a_pairs = [None] * (pg // 2)
    for p in reversed(range(pg // 2)):
        a_pairs[p], c = _sb_pair(z[:, p * PAIR:(p + 1) * PAIR] + bias, None, c, wm)
    c_ref[...] = c
    acc_ref[...] += lax.dot_general(jnp.concatenate(a_pairs, axis=1), vb_ref[...], _NT,
                                    preferred_element_type=F32)

    @pl.when(s == n_steps - 1)
    def _():
        m = jnp.where(own_head, acc_ref[...], 0.0)
        out = m[0:t_new, :]
        for hh in range(1, N_HEADS):
            out = out + m[hh * t_new:(hh + 1) * t_new, :]
        o_ref[...] = out


def _sattn_call(q, k_new, v_new, bias_rows, cache_k, cache_v, page_table, t_new):
    n, d = q.shape
    batch = n // t_new
    n_pages = page_table.shape[1]
    pg = 8
    n_steps = n_pages // pg
    rows = N_HEADS * t_new

    tok = pl.BlockSpec((t_new, d), lambda b, s, pt: (b, 0))

    def page_spec(p):
        return pl.BlockSpec((1, N_HEADS, HEAD_DIM, KEY_TILE),
                            lambda b, s, pt, p=p: (pt[b, (n_steps - 1 - s) * pg + p], 0, 0, 0))

    grid_spec = pltpu.PrefetchScalarGridSpec(
        num_scalar_prefetch=1,
        grid=(batch, n_steps),
        in_specs=[tok, tok, tok,
                  pl.BlockSpec((rows, PAIR), lambda b, s, pt: (0, 0)),
                  pl.BlockSpec((PAIR, 2 * PAIR), lambda b, s, pt: (0, 0))]
                 + [page_spec(p) for p in range(pg)] * 2,
        out_specs=tok,
        scratch_shapes=[pltpu.VMEM((rows, d), BF16),
                        pltpu.VMEM((d, pg * KEY_TILE), BF16),
                        pltpu.VMEM((d, pg * KEY_TILE), BF16),
                        pltpu.VMEM((rows, KEY_TILE), F32),
                        pltpu.VMEM((rows, d), F32)],
    )
    return pl.pallas_call(
        functools.partial(_sattn_body, pg, n_steps, t_new),
        grid_spec=grid_spec,
        out_shape=jax.ShapeDtypeStruct((n, d), F32),
        compiler_params=pltpu.CompilerParams(
            dimension_semantics=("arbitrary", "arbitrary"),
            vmem_limit_bytes=48 * 1024 * 1024),
        name="sample_attn",
    )(page_table, q, k_new, v_new, bias_rows, _pair_suffix_matrix(),
      *([cache_k] * pg), *([cache_v] * pg))


def _post_body(o_ref, sza_ref, sga_ref, gc_ref, sgc_ref, x_ref, p_ref, woa_ref, woc_ref, wo_ref,
               wpg_ref, wpl_ref, pnw_ref, y_ref):
    ga = (o_ref[...] * sza_ref[...]).astype(BF16)
    y_a = jnp.dot(ga, woa_ref[...], preferred_element_type=F32)
    y_c = jnp.dot(gc_ref[...], woc_ref[...], preferred_element_type=F32)
    merged = sgc_ref[...] * y_c + sga_ref[...] * y_a
    x1 = x_ref[...] + jnp.dot(merged.astype(BF16), wo_ref[...], preferred_element_type=F32)
    ms = jnp.mean(x1 * x1, axis=-1, keepdims=True)
    nrm = (x1 * lax.rsqrt(ms + EPS) * pnw_ref[...]).astype(BF16)
    gate = _sigmoid(jnp.dot(nrm, wpg_ref[...], preferred_element_type=F32))
    ple = jnp.dot(p_ref[...].astype(BF16), wpl_ref[...], preferred_element_type=F32)
    y_ref[...] = x1 + ple * gate


def _post_call(o, sza, sga, gc, sgc, x, p, woa_bf, woc_bf, wo_bf, wpg_bf, wpl_bf, pnw):
    n, d = x.shape
    pd = p.shape[-1]
    tm = min(512, n)
    tokd = pl.BlockSpec((tm, d), lambda i: (i, 0))
    full = lambda shape: pl.BlockSpec(shape, lambda i: (0, 0))
    return pl.pallas_call(
        _post_body,
        grid=(n // tm,),
        in_specs=[tokd, tokd, tokd, tokd, tokd, tokd, pl.BlockSpec((tm, pd), lambda i: (i, 0)),
                  full((d, d)), full((d, d)), full((d, d)), full((d, d)), full((pd, d)), full((1, d))],
        out_specs=tokd,
        out_shape=jax.ShapeDtypeStruct((n, d), F32),
        compiler_params=pltpu.CompilerParams(
            dimension_semantics=("arbitrary",),
            vmem_limit_bytes=56 * 1024 * 1024),
        name="post",
    )(o, sza, sga, gc, sgc, x, p, woa_bf, woc_bf, wo_bf, wpg_bf, wpl_bf, pnw)


def kernel(x_prompt, x_sample, p_prompt, p_sample, cache_k, cache_v, state_conv, page_table, norm_w, w_in, conv_w, q_norm_w, k_norm_w, attn_bias, w_out_conv, w_out_attn, w_o, w_ple, ple_norm_w, w_ple_gate):
    depth = norm_w.shape[0]
    batch, seq_len, d = x_prompt.shape
    dec_batch, dec_seq, _ = x_sample.shape
    page = cache_k.shape[2]
    assert page == KEY_TILE and d == N_HEADS * HEAD_DIM and dec_seq == CONV_TAIL
    n_p = batch * seq_len
    n_s = dec_batch * dec_seq
    cb = 512

    hp = x_prompt.reshape(n_p, d)
    hs = x_sample.reshape(n_s, d)
    outs = [[] for _ in range(6)]
    for l in range(depth):
        w_in_bf = w_in[l].astype(BF16)
        woc_bf = w_out_conv[l].astype(BF16)
        woa_bf = w_out_attn[l].astype(BF16)
        wo_bf = w_o[l].astype(BF16)
        wpg_bf = w_ple_gate[l].astype(BF16)
        wpl_bf = w_ple[l].astype(BF16)
        nw = norm_w[l].reshape(1, d)
        pnw = ple_norm_w[l].reshape(1, d)
        qw_t = jnp.tile(q_norm_w[l], cb // HEAD_DIM).reshape(1, cb)
        kw_t = jnp.tile(k_norm_w[l], cb // HEAD_DIM).reshape(1, cb)
        bias2 = attn_bias[l] * LOG2E
        b_hi = bias2.astype(BF16)
        b_lo = (bias2 - b_hi.astype(F32)).astype(BF16)
        pair_lanes = lambda b: jnp.repeat(b.reshape(N_HEADS // 2, 2), KEY_TILE, axis=1)[:, None, :]
        bias_rows_p = jnp.concatenate(
            [pair_lanes(b_hi), pair_lanes(b_lo),
             jnp.zeros((N_HEADS // 2, KEY_TILE - 2, PAIR), BF16)], axis=1)
        bias_rows_s = jnp.broadcast_to(jnp.repeat(bias2, dec_seq)[:, None], (N_HEADS * dec_seq, PAIR))

        qb, kt, vt, gc, sgc, sza, sga, tails = _pre_call(
            hp, nw, w_in_bf, conv_w[l], qw_t, kw_t, cb, seq_len)
        o = _pattn_call(qb, kt, vt, bias_rows_p, batch, seq_len).reshape(n_p, d)
        hp = _post_call(o, sza, sga, gc, sgc, hp, p_prompt[l].reshape(n_p, -1),
                        woa_bf, woc_bf, wo_bf, wpg_bf, wpl_bf, pnw)
        to_bthd = lambda a: a.reshape(batch, N_HEADS, HEAD_DIM, seq_len).transpose(0, 3, 1, 2)
        outs[0].append(to_bthd(kt))
        outs[1].append(to_bthd(vt))
        outs[2].append(tails.reshape(batch, -1, CONV_TAIL, d)[:, -1, CONV_TAIL - 2:, :])

        st = state_conv[l]
        zrow = jnp.zeros((dec_batch, dec_seq, d), F32)
        hist1 = jnp.concatenate([st[:, 1:2], zrow[:, 1:]], axis=1).reshape(n_s, d)
        hist2 = jnp.concatenate([st[:, 0:2], zrow[:, 2:]], axis=1).reshape(n_s, d)
        qb, kf, vf, gc, sgc, sza, sga, u_s = _pre_call(
            hs, nw, w_in_bf, conv_w[l], qw_t, kw_t, cb, dec_seq, hist=(hist1, hist2))
        o = _sattn_call(qb.astype(F32), kf, vf, bias_rows_s,
                        cache_k[l].transpose(0, 2, 3, 1), cache_v[l].transpose(0, 2, 3, 1),
                        page_table, dec_seq)
        hs = _post_call(o, sza, sga, gc, sgc, hs, p_sample[l].reshape(n_s, -1),
                        woa_bf, woc_bf, wo_bf, wpg_bf, wpl_bf, pnw)
        outs[3].append(kf.reshape(dec_batch, dec_seq, N_HEADS, HEAD_DIM))
        outs[4].append(vf.reshape(dec_batch, dec_seq, N_HEADS, HEAD_DIM))
        outs[5].append(u_s.reshape(dec_batch, dec_seq, d)[:, dec_seq - 2:, :])

    return (hp.reshape(batch, seq_len, d), hs.reshape(dec_batch, dec_seq, d),
            jnp.stack(outs[0]), jnp.stack(outs[1]), jnp.stack(outs[2]),
            jnp.stack(outs[3]), jnp.stack(outs[4]), jnp.stack(outs[5]))
```

```python
import functools

import numpy as np
import jax
import jax.numpy as jnp
from jax import lax
from jax.experimental import pallas as pl
from jax.experimental.pallas import tpu as pltpu

F32 = jnp.float32
BF16 = jnp.bfloat16

N_HEADS = 16
HEAD_DIM = 64
EPS = 1e-6
LOG2E = 1.4426950408889634
N_SPLITS = 10
KEY_TILE = 128
PAIR = 2 * KEY_TILE
CONV_TAIL = 8
Q_CHUNK = 512
PAGES_PER_STEP = 16

_NT = (((1,), (1,)), ((), ()))


def _sigmoid(z):
    return 1.0 / (1.0 + jnp.exp(-z))


def _silu(z):
    return z * _sigmoid(z)


def _pair_suffix_matrix():
    k = np.arange(PAIR)[:, None]
    c = np.arange(PAIR)[None, :]
    return jnp.asarray(-(k > c).astype(np.float32), dtype=BF16)


def _head_sum_matrix(cb):
    r = np.arange(cb)[:, None] // HEAD_DIM
    c = np.arange(cb)[None, :] // HEAD_DIM
    return jnp.asarray(r == c, dtype=BF16)


def _neg_abs(z):
    bits = lax.bitcast_convert_type(z, jnp.uint32) | jnp.uint32(0x80000000)
    return lax.bitcast_convert_type(bits, F32)


def _sb_pair(z, mask, c, wm):
    e = jnp.exp2(_neg_abs(z))
    sp = jnp.maximum(z, 0.0) + jnp.log(1.0 + e) * LOG2E
    log_beta = z - sp
    if mask is not None:
        sp = jnp.where(mask, sp, 0.0)
    rev = jnp.dot(sp.astype(BF16), wm, preferred_element_type=F32)
    a = jnp.exp2(log_beta + rev + jnp.concatenate([c, c], axis=1))
    if mask is not None:
        a = jnp.where(mask, a, 0.0)
    return a.astype(BF16), c - jnp.sum(sp, axis=-1, keepdims=True)


def _pre_body(is_prompt, tm, cb, tiles_per_seq, *refs):
    x_ref, nw_ref = refs[0:2]
    w_bc, w_cc, w_xc, w_zc, w_q, w_k, w_v, w_za, w_gc, w_ga = refs[2:12]
    cw_ref, qw_ref, kw_ref, g_ref = refs[12:16]
    pos = 16
    if not is_prompt:
        h1_ref, h2_ref = refs[pos:pos + 2]
        pos += 2
    (qb_ref, k_ref, v_ref, gc_ref, sgc_ref, sza_ref, sga_ref, tail_ref) = refs[pos:pos + 8]
    pos += 8
    if is_prompt:
        carry_ref = refs[pos]

    i = pl.program_id(1)

    x = x_ref[...]
    ms = jnp.mean(x * x, axis=-1, keepdims=True)
    h = (x * lax.rsqrt(ms + EPS) * nw_ref[...]).astype(BF16)

    def proj(w_ref):
        return jnp.dot(h, w_ref[...], preferred_element_type=F32)

    u = proj(w_cc) * proj(w_xc)
    row = lax.broadcasted_iota(jnp.int32, (tm, cb), 0)
    r1 = pltpu.roll(u, 1, 0)
    r2 = pltpu.roll(u, 2, 0)
    if is_prompt:
        @pl.when(i % tiles_per_seq == 0)
        def _():
            carry_ref[...] = jnp.zeros((CONV_TAIL, cb), F32)

        cr = carry_ref[...]
        c6 = cr[CONV_TAIL - 2:CONV_TAIL - 1, :]
        c7 = cr[CONV_TAIL - 1:CONV_TAIL, :]
        u1 = jnp.where(row >= 1, r1, c7)
        u2 = jnp.where(row >= 2, r2, jnp.where(row == 0, c6, c7))
        tail = u[tm - CONV_TAIL:, :]
        carry_ref[...] = tail
        tail_ref[0] = tail
    else:
        t = row & 7
        u1 = jnp.where(t >= 1, r1, h1_ref[...])
        u2 = jnp.where(t >= 2, r2, h2_ref[...])
        tail_ref[...] = u
    conv = cw_ref[2:3, :] * u + cw_ref[1:2, :] * u1 + cw_ref[0:1, :] * u2
    gc_ref[...] = (proj(w_bc) * conv * _silu(proj(w_zc))).astype(BF16)
    sgc_ref[...] = _sigmoid(proj(w_gc))

    gm = g_ref[...]

    def head_norm(v, w_row):
        ss = jnp.dot((v * v).astype(BF16), gm, preferred_element_type=F32)
        return v * lax.rsqrt(ss * (1.0 / HEAD_DIM) + EPS) * w_row

    qn = head_norm(proj(w_q), qw_ref[...])
    qb_ref[...] = (qn * (HEAD_DIM ** -0.5 * LOG2E)).astype(BF16)
    kn = head_norm(proj(w_k), kw_ref[...])
    v = proj(w_v)
    if is_prompt:
        k_ref[0] = kn.T
        v_ref[0] = v.T
    else:
        k_ref[...] = kn
        v_ref[...] = v
    sza_ref[...] = _silu(proj(w_za))
    sga_ref[...] = _sigmoid(proj(w_ga))


def _pre_call(x, norm_w, w_in_bf, conv_w, qw_t, kw_t, cb, seq_len, hist=None):
    n, d = x.shape
    is_prompt = hist is None
    n_cb = d // cb
    tm = 512 if is_prompt else n
    n_tiles = n // tm
    tiles_per_seq = seq_len // tm if is_prompt else 1

    tok = lambda j, i: (i, j)
    in_specs = [
        pl.BlockSpec((tm, d), lambda j, i: (i, 0)),
        pl.BlockSpec((1, d), lambda j, i: (0, 0)),
    ]
    for s in range(N_SPLITS):
        in_specs.append(pl.BlockSpec((d, cb), functools.partial(lambda j, i, s: (0, s * n_cb + j), s=s)))
    in_specs += [
        pl.BlockSpec((3, cb), lambda j, i: (0, j)),
        pl.BlockSpec((1, cb), lambda j, i: (0, 0)),
        pl.BlockSpec((1, cb), lambda j, i: (0, 0)),
        pl.BlockSpec((cb, cb), lambda j, i: (0, 0)),
    ]
    args = [x, norm_w, *([w_in_bf] * N_SPLITS), conv_w, qw_t, kw_t, _head_sum_matrix(cb)]
    if not is_prompt:
        in_specs += [pl.BlockSpec((tm, cb), tok), pl.BlockSpec((tm, cb), tok)]
        args += list(hist)

    if is_prompt:
        kv_shape = jax.ShapeDtypeStruct((n // seq_len, d, seq_len), F32)
        kv_spec = pl.BlockSpec((1, cb, tm), lambda j, i: (i // tiles_per_seq, j, i % tiles_per_seq))
        tail_shape = jax.ShapeDtypeStruct((n_tiles, CONV_TAIL, d), F32)
        tail_spec = pl.BlockSpec((1, CONV_TAIL, cb), lambda j, i: (i, 0, j))
    else:
        kv_shape = jax.ShapeDtypeStruct((n, d), F32)
        kv_spec = pl.BlockSpec((tm, cb), tok)
        tail_shape = jax.ShapeDtypeStruct((n, d), F32)
        tail_spec = pl.BlockSpec((tm, cb), tok)
    tok_spec = pl.BlockSpec((tm, cb), tok)
    out_shape = [jax.ShapeDtypeStruct((n, d), BF16), kv_shape, kv_shape,
                 jax.ShapeDtypeStruct((n, d), BF16),
                 jax.ShapeDtypeStruct((n, d), F32), jax.ShapeDtypeStruct((n, d), F32),
                 jax.ShapeDtypeStruct((n, d), F32), tail_shape]
    out_specs = [tok_spec, kv_spec, kv_spec, tok_spec, tok_spec, tok_spec, tok_spec, tail_spec]

    return pl.pallas_call(
        functools.partial(_pre_body, is_prompt, tm, cb, tiles_per_seq),
        grid=(n_cb, n_tiles),
        in_specs=in_specs,
        out_specs=out_specs,
        out_shape=out_shape,
        scratch_shapes=[pltpu.VMEM((CONV_TAIL, cb), F32)] if is_prompt else [],
        compiler_params=pltpu.CompilerParams(
            dimension_semantics=("arbitrary", "arbitrary"),
            vmem_limit_bytes=56 * 1024 * 1024),
        name="pre_prompt" if is_prompt else "pre_sample",
    )(*args)


def _pattn_body(nt, q_ref, k_ref, v_ref, brow_ref, wm_ref, o_ref, kk_ref, vv_ref):
    t = nt * KEY_TILE
    chan = lax.broadcasted_iota(jnp.int32, (2 * HEAD_DIM, t), 0)
    first = chan < HEAD_DIM
    k2 = k_ref[0]
    v2 = v_ref[0]
    k_h0 = jnp.where(first, k2, 0.0).astype(BF16)
    k_h1 = jnp.where(first, 0.0, k2).astype(BF16)
    v_h0 = jnp.where(first, v2, 0.0).astype(BF16)
    v_h1 = jnp.where(first, 0.0, v2).astype(BF16)
    b1 = brow_ref[0]
    brow = jnp.concatenate([b1[:, :KEY_TILE], b1[:, :KEY_TILE], b1[:, KEY_TILE:], b1[:, KEY_TILE:]],
                           axis=1)
    for p in range(nt // 2):
        cols = slice(p * PAIR, (p + 1) * PAIR)
        kk_ref[p, 0:KEY_TILE, 0:PAIR] = k_h0[:, cols]
        kk_ref[p, 0:KEY_TILE, PAIR:] = k_h1[:, cols]
        kk_ref[p, KEY_TILE:, :] = brow
        vv_ref[p, :, 0:PAIR] = v_h0[:, cols]
        vv_ref[p, :, PAIR:] = v_h1[:, cols]

    wm = wm_ref[...]
    ones = jnp.ones((Q_CHUNK, KEY_TILE), BF16)

    def pair(q_rows, p, mask, c):
        m = q_rows.shape[0]
        z4 = jnp.dot(q_rows, kk_ref[p], preferred_element_type=F32)
        z = jnp.concatenate([z4[:, :PAIR], z4[:, PAIR:]], axis=0)
        a, c = _sb_pair(z, mask, c, wm)
        a4 = jnp.concatenate([a[:m], a[m:]], axis=1)
        return lax.dot_general(a4, vv_ref[p], _NT, preferred_element_type=F32), c

    def band_mask(m):
        rr = lax.broadcasted_iota(jnp.int32, (m, PAIR), 0)
        cc = lax.broadcasted_iota(jnp.int32, (m, PAIR), 1)
        mk = cc < rr
        return jnp.concatenate([mk, mk], axis=0)

    half = Q_CHUNK // 2
    mask_upper = band_mask(half)
    mask_all = band_mask(Q_CHUNK)

    for ic in range(t // Q_CHUNK):
        r0 = ic * Q_CHUNK
        q_c = jnp.concatenate([q_ref[0, r0:r0 + Q_CHUNK, :], ones], axis=1)
        zc = jnp.zeros((Q_CHUNK, KEY_TILE), F32)
        o_up, c_up = pair(q_c[half:], 2 * ic + 1, mask_upper, zc)
        c0 = jnp.concatenate([zc[:half], c_up[:half], zc[:half], c_up[half:]], axis=0)
        o_c, cst = pair(q_c, 2 * ic, mask_all, c0)
        o_c = o_c + jnp.concatenate([jnp.zeros((half, 2 * HEAD_DIM), F32), o_up], axis=0)
        for p in reversed(range(2 * ic)):
            o_new, cst = pair(q_c, p, None, cst)
            o_c = o_c + o_new
        o_ref[0, r0:r0 + Q_CHUNK, :] = o_c


def _pattn_call(qb, kt, vt, bias_rows, batch, seq_len):
    d = qb.shape[-1]
    n_hp = d // (2 * HEAD_DIM)
    nt = seq_len // KEY_TILE
    assert seq_len % Q_CHUNK == 0 and Q_CHUNK == 2 * PAIR
    blk = pl.BlockSpec((1, seq_len, 2 * HEAD_DIM), lambda b, hp: (b, 0, hp))
    blk_t = pl.BlockSpec((1, 2 * HEAD_DIM, seq_len), lambda b, hp: (b, hp, 0))
    return pl.pallas_call(
        functools.partial(_pattn_body, nt),
        grid=(batch, n_hp),
        in_specs=[blk, blk_t, blk_t,
                  pl.BlockSpec((1, KEY_TILE, PAIR), lambda b, hp: (hp, 0, 0)),
                  pl.BlockSpec((PAIR, PAIR), lambda b, hp: (0, 0))],
        out_specs=blk,
        out_shape=jax.ShapeDtypeStruct((batch, seq_len, d), F32),
        scratch_shapes=[pltpu.VMEM((nt // 2, PAIR, 2 * PAIR), BF16),
                        pltpu.VMEM((nt // 2, 2 * HEAD_DIM, 2 * PAIR), BF16)],
        compiler_params=pltpu.CompilerParams(
            dimension_semantics=("arbitrary", "arbitrary"),
            vmem_limit_bytes=48 * 1024 * 1024),
        name="prompt_attn",
    )(qb.reshape(batch, seq_len, d), kt, vt, bias_rows, _pair_suffix_matrix())


def _sattn_body(pg, n_steps, t_new, pt_ref, q_ref, kn_ref, vn_ref, b_ref, wm_ref, *rest):
    k_refs = rest[:pg]
    v_refs = rest[pg:2 * pg]
    o_ref = rest[2 * pg]
    qr_ref, kb_ref, vb_ref, c_ref, acc_ref = rest[2 * pg + 1:]
    d = q_ref.shape[-1]
    rows = N_HEADS * t_new
    s = pl.program_id(1)
    wm = wm_ref[...]
    bias = b_ref[...]
    row = lax.broadcasted_iota(jnp.int32, (rows, d), 0)
    lane = lax.broadcasted_iota(jnp.int32, (rows, d), 1)
    own_head = (row // t_new) == (lane // HEAD_DIM)

    @pl.when(s == 0)
    def _():
        qt = jnp.concatenate([q_ref[...]] * N_HEADS, axis=0)
        qr = jnp.where(own_head, qt, 0.0).astype(BF16)
        qr_ref[...] = qr
        pad = jnp.zeros((PAIR - t_new, d), F32)
        kn = jnp.concatenate([kn_ref[...], pad], axis=0).astype(BF16)
        vn = jnp.concatenate([vn_ref[...], pad], axis=0).astype(BF16)
        z = lax.dot_general(qr, kn, _NT, preferred_element_type=F32) + bias
        rr = lax.broadcasted_iota(jnp.int32, (rows, PAIR), 0)
        ll = lax.broadcasted_iota(jnp.int32, (rows, PAIR), 1)
        mask = ll < (rr % t_new)
        a, c = _sb_pair(z, mask, jnp.zeros((rows, KEY_TILE), F32), wm)
        c_ref[...] = c
        acc_ref[...] = jnp.dot(a, vn, preferred_element_type=F32)

    for p in range(pg):
        kb_ref[:, p * KEY_TILE:(p + 1) * KEY_TILE] = k_refs[p][0].reshape(d, KEY_TILE).astype(BF16)
        vb_ref[:, p * KEY_TILE:(p + 1) * KEY_TILE] = v_refs[p][0].reshape(d, KEY_TILE).astype(BF16)
    z = jnp.dot(qr_ref[...], kb_ref[...], preferred_element_type=F32)
    c = c_ref[...]
    a_pairs = [None] * (pg // 2)
    for p in reversed(range(pg // 2)):
        a_pairs[p], c = _sb_pair(z[:, p * PAIR:(p + 1) * PAIR] + bias, None, c, wm)
    c_ref[...] = c
    acc_ref[...] += lax.dot_general(jnp.concatenate(a_pairs, axis=1), vb_ref[...], _NT,
                                    preferred_element_type=F32)

    @pl.when(s == n_steps - 1)
    def _():
        m = jnp.where(own_head, acc_ref[...], 0.0)
        out = m[0:t_new, :]
        for hh in range(1, N_HEADS):
            out = out + m[hh * t_new:(hh + 1) * t_new, :]
        o_ref[...] = out


def _sattn_call(q, k_new, v_new, bias_rows, cache_k, cache_v, page_table, t_new):
    n, d = q.shape
    batch = n // t_new
    n_pages = page_table.shape[1]
    pg = PAGES_PER_STEP
    assert n_pages % pg == 0 and pg % 2 == 0
    n_steps = n_pages // pg
    rows = N_HEADS * t_new

    tok = pl.BlockSpec((t_new, d), lambda b, s, pt: (b, 0))

    def page_spec(p):
        return pl.BlockSpec((1, N_HEADS, HEAD_DIM, KEY_TILE),
                            lambda b, s, pt, p=p: (pt[b, (n_steps - 1 - s) * pg + p], 0, 0, 0))

    grid_spec = pltpu.PrefetchScalarGridSpec(
        num_scalar_prefetch=1,
        grid=(batch, n_steps),
        in_specs=[tok, tok, tok,
                  pl.BlockSpec((rows, PAIR), lambda b, s, pt: (0, 0)),
                  pl.BlockSpec((PAIR, PAIR), lambda b, s, pt: (0, 0))]
                 + [page_spec(p) for p in range(pg)] * 2,
        out_specs=tok,
        scratch_shapes=[pltpu.VMEM((rows, d), BF16),
                        pltpu.VMEM((d, pg * KEY_TILE), BF16),
                        pltpu.VMEM((d, pg * KEY_TILE), BF16),
                        pltpu.VMEM((rows, KEY_TILE), F32),
                        pltpu.VMEM((rows, d), F32)],
    )
    return pl.pallas_call(
        functools.partial(_sattn_body, pg, n_steps, t_new),
        grid_spec=grid_spec,
        out_shape=jax.ShapeDtypeStruct((n, d), F32),
        compiler_params=pltpu.CompilerParams(
            dimension_semantics=("arbitrary", "arbitrary"),
            vmem_limit_bytes=56 * 1024 * 1024),
        name="sample_attn",
    )(page_table, q, k_new, v_new, bias_rows, _pair_suffix_matrix(),
      *([cache_k] * pg), *([cache_v] * pg))


def _post_body(o_ref, sza_ref, sga_ref, gc_ref, sgc_ref, x_ref, p_ref, woa_ref, woc_ref, wo_ref,
               wpg_ref, wpl_ref, pnw_ref, y_ref):
    ga = (o_ref[...] * sza_ref[...]).astype(BF16)
    y_a = jnp.dot(ga, woa_ref[...], preferred_element_type=F32)
    y_c = jnp.dot(gc_ref[...], woc_ref[...], preferred_element_type=F32)
    merged = sgc_ref[...] * y_c + sga_ref[...] * y_a
    x1 = x_ref[...] + jnp.dot(merged.astype(BF16), wo_ref[...], preferred_element_type=F32)
    ms = jnp.mean(x1 * x1, axis=-1, keepdims=True)
    nrm = (x1 * lax.rsqrt(ms + EPS) * pnw_ref[...]).astype(BF16)
    gate = _sigmoid(jnp.dot(nrm, wpg_ref[...], preferred_element_type=F32))
    ple = jnp.dot(p_ref[...].astype(BF16), wpl_ref[...], preferred_element_type=F32)
    y_ref[...] = x1 + ple * gate


def _post_call(o, sza, sga, gc, sgc, x, p, woa_bf, woc_bf, wo_bf, wpg_bf, wpl_bf, pnw):
    n, d = x.shape
    pd = p.shape[-1]
    tm = min(512, n)
    tokd = pl.BlockSpec((tm, d), lambda i: (i, 0))
    full = lambda shape: pl.BlockSpec(shape, lambda i: (0, 0))
    return pl.pallas_call(
        _post_body,
        grid=(n // tm,),
        in_specs=[tokd, tokd, tokd, tokd, tokd, tokd, pl.BlockSpec((tm, pd), lambda i: (i, 0)),
                  full((d, d)), full((d, d)), full((d, d)), full((d, d)), full((pd, d)), full((1, d))],
        out_specs=tokd,
        out_shape=jax.ShapeDtypeStruct((n, d), F32),
        compiler_params=pltpu.CompilerParams(
            dimension_semantics=("arbitrary",),
            vmem_limit_bytes=56 * 1024 * 1024),
        name="post",
    )(o, sza, sga, gc, sgc, x, p, woa_bf, woc_bf, wo_bf, wpg_bf, wpl_bf, pnw)


def kernel(x_prompt, x_sample, p_prompt, p_sample, cache_k, cache_v, state_conv, page_table, norm_w, w_in, conv_w, q_norm_w, k_norm_w, attn_bias, w_out_conv, w_out_attn, w_o, w_ple, ple_norm_w, w_ple_gate):
    depth = norm_w.shape[0]
    batch, seq_len, d = x_prompt.shape
    dec_batch, dec_seq, _ = x_sample.shape
    page = cache_k.shape[2]
    assert page == KEY_TILE and d == N_HEADS * HEAD_DIM and dec_seq == CONV_TAIL
    n_p = batch * seq_len
    n_s = dec_batch * dec_seq
    cb = 512

    hp = x_prompt.reshape(n_p, d)
    hs = x_sample.reshape(n_s, d)
    outs = [[] for _ in range(6)]
    for l in range(depth):
        w_in_bf = w_in[l].astype(BF16)
        woc_bf = w_out_conv[l].astype(BF16)
        woa_bf = w_out_attn[l].astype(BF16)
        wo_bf = w_o[l].astype(BF16)
        wpg_bf = w_ple_gate[l].astype(BF16)
        wpl_bf = w_ple[l].astype(BF16)
        nw = norm_w[l].reshape(1, d)
        pnw = ple_norm_w[l].reshape(1, d)
        qw_t = jnp.tile(q_norm_w[l], cb // HEAD_DIM).reshape(1, cb)
        kw_t = jnp.tile(k_norm_w[l], cb // HEAD_DIM).reshape(1, cb)
        bias2 = attn_bias[l] * LOG2E
        b_hi = bias2.astype(BF16)
        b_lo = (bias2 - b_hi.astype(F32)).astype(BF16)
        pair_lanes = lambda b: jnp.repeat(b.reshape(N_HEADS // 2, 2), KEY_TILE, axis=1)[:, None, :]
        bias_rows_p = jnp.concatenate(
            [pair_lanes(b_hi), pair_lanes(b_lo),
             jnp.zeros((N_HEADS // 2, KEY_TILE - 2, PAIR), BF16)], axis=1)
        bias_rows_s = jnp.broadcast_to(jnp.repeat(bias2, dec_seq)[:, None], (N_HEADS * dec_seq, PAIR))

        qb, kt, vt, gc, sgc, sza, sga, tails = _pre_call(
            hp, nw, w_in_bf, conv_w[l], qw_t, kw_t, cb, seq_len)
        o = _pattn_call(qb, kt, vt, bias_rows_p, batch, seq_len).reshape(n_p, d)
        hp = _post_call(o, sza, sga, gc, sgc, hp, p_prompt[l].reshape(n_p, -1),
                        woa_bf, woc_bf, wo_bf, wpg_bf, wpl_bf, pnw)
        to_bthd = lambda a: a.reshape(batch, N_HEADS, HEAD_DIM, seq_len).transpose(0, 3, 1, 2)
        outs[0].append(to_bthd(kt))
        outs[1].append(to_bthd(vt))
        outs[2].append(tails.reshape(batch, -1, CONV_TAIL, d)[:, -1, CONV_TAIL - 2:, :])

        st = state_conv[l]
        zrow = jnp.zeros((dec_batch, dec_seq, d), F32)
        hist1 = jnp.concatenate([st[:, 1:2], zrow[:, 1:]], axis=1).reshape(n_s, d)
        hist2 = jnp.concatenate([st[:, 0:2], zrow[:, 2:]], axis=1).reshape(n_s, d)
        qb, kf, vf, gc, sgc, sza, sga, u_s = _pre_call(
            hs, nw, w_in_bf, conv_w[l], qw_t, kw_t, cb, dec_seq, hist=(hist1, hist2))
        o = _sattn_call(qb.astype(F32), kf, vf, bias_rows_s,
                        cache_k[l].transpose(0, 2, 3, 1), cache_v[l].transpose(0, 2, 3, 1),
                        page_table, dec_seq)
        hs = _post_call(o, sza, sga, gc, sgc, hs, p_sample[l].reshape(n_s, -1),
                        woa_bf, woc_bf, wo_bf, wpg_bf, wpl_bf, pnw)
        outs[3].append(kf.reshape(dec_batch, dec_seq, N_HEADS, HEAD_DIM))
        outs[4].append(vf.reshape(dec_batch, dec_seq, N_HEADS, HEAD_DIM))
        outs[5].append(u_s.reshape(dec_batch, dec_seq, d)[:, dec_seq - 2:, :])

    return (hp.reshape(batch, seq_len, d), hs.reshape(dec_batch, dec_seq, d),
            jnp.stack(outs[0]), jnp.stack(outs[1]), jnp.stack(outs[2]),
            jnp.stack(outs[3]), jnp.stack(outs[4]), jnp.stack(outs[5]))
```

```python
import functools

import numpy as np
import jax
import jax.numpy as jnp
from jax import lax
from jax.experimental import pallas as pl
from jax.experimental.pallas import tpu as pltpu

F32 = jnp.float32
BF16 = jnp.bfloat16

N_HEADS = 16
HEAD_DIM = 64
EPS = 1e-6
LOG2E = 1.4426950408889634
N_SPLITS = 10
KEY_TILE = 128
PAIR = 2 * KEY_TILE
CONV_TAIL = 8
Q_CHUNK = 512
PAGES_PER_STEP = 16
SAMPLE_GROUP = 8

_NT = (((1,), (1,)), ((), ()))


def _sigmoid(z):
    return 1.0 / (1.0 + jnp.exp(-z))


def _silu(z):
    return z * _sigmoid(z)


def _pair_suffix_matrix():
    k = np.arange(PAIR)[:, None]
    c = np.arange(PAIR)[None, :]
    return jnp.asarray(-(k > c).astype(np.float32), dtype=BF16)


def _head_sum_matrix(cb):
    r = np.arange(cb)[:, None] // HEAD_DIM
    c = np.arange(cb)[None, :] // HEAD_DIM
    return jnp.asarray(r == c, dtype=BF16)


def _neg_abs(z):
    bits = lax.bitcast_convert_type(z, jnp.uint32) | jnp.uint32(0x80000000)
    return lax.bitcast_convert_type(bits, F32)


def _sb_pair(z, mask, c, wm):
    e = jnp.exp2(_neg_abs(z))
    sp = jnp.maximum(z, 0.0) + jnp.log(1.0 + e) * LOG2E
    log_beta = z - sp
    if mask is not None:
        sp = jnp.where(mask, sp, 0.0)
    rev = jnp.dot(sp.astype(BF16), wm, preferred_element_type=F32)
    a = jnp.exp2(log_beta + rev + jnp.concatenate([c, c], axis=1))
    if mask is not None:
        a = jnp.where(mask, a, 0.0)
    return a.astype(BF16), c - jnp.sum(sp, axis=-1, keepdims=True)


def _pre_body(is_prompt, tm, cb, tiles_per_seq, *refs):
    x_ref, nw_ref = refs[0:2]
    w_bc, w_cc, w_xc, w_zc, w_q, w_k, w_v, w_za, w_gc, w_ga = refs[2:12]
    cw_ref, qw_ref, kw_ref, g_ref = refs[12:16]
    pos = 16
    if not is_prompt:
        h1_ref, h2_ref = refs[pos:pos + 2]
        pos += 2
    (qb_ref, k_ref, v_ref, gc_ref, sgc_ref, sza_ref, sga_ref, tail_ref) = refs[pos:pos + 8]
    pos += 8
    if is_prompt:
        carry_ref = refs[pos]

    i = pl.program_id(1)

    x = x_ref[...]
    ms = jnp.mean(x * x, axis=-1, keepdims=True)
    h = (x * lax.rsqrt(ms + EPS) * nw_ref[...]).astype(BF16)

    def proj(w_ref):
        return jnp.dot(h, w_ref[...], preferred_element_type=F32)

    u = proj(w_cc) * proj(w_xc)
    row = lax.broadcasted_iota(jnp.int32, (tm, cb), 0)
    r1 = pltpu.roll(u, 1, 0)
    r2 = pltpu.roll(u, 2, 0)
    if is_prompt:
        @pl.when(i % tiles_per_seq == 0)
        def _():
            carry_ref[...] = jnp.zeros((CONV_TAIL, cb), F32)

        cr = carry_ref[...]
        c6 = cr[CONV_TAIL - 2:CONV_TAIL - 1, :]
        c7 = cr[CONV_TAIL - 1:CONV_TAIL, :]
        u1 = jnp.where(row >= 1, r1, c7)
        u2 = jnp.where(row >= 2, r2, jnp.where(row == 0, c6, c7))
        tail = u[tm - CONV_TAIL:, :]
        carry_ref[...] = tail
        tail_ref[0] = tail
    else:
        t = row & 7
        u1 = jnp.where(t >= 1, r1, h1_ref[...])
        u2 = jnp.where(t >= 2, r2, h2_ref[...])
        tail_ref[...] = u
    conv = cw_ref[2:3, :] * u + cw_ref[1:2, :] * u1 + cw_ref[0:1, :] * u2
    gc_ref[...] = (proj(w_bc) * conv * _silu(proj(w_zc))).astype(BF16)
    sgc_ref[...] = _sigmoid(proj(w_gc))

    gm = g_ref[...]

    def head_norm(v, w_row):
        ss = jnp.dot((v * v).astype(BF16), gm, preferred_element_type=F32)
        return v * lax.rsqrt(ss * (1.0 / HEAD_DIM) + EPS) * w_row

    qn = head_norm(proj(w_q), qw_ref[...])
    qb_ref[...] = (qn * (HEAD_DIM ** -0.5 * LOG2E)).astype(BF16)
    kn = head_norm(proj(w_k), kw_ref[...])
    v = proj(w_v)
    if is_prompt:
        k_ref[0] = kn.T
        v_ref[0] = v.T
    else:
        k_ref[...] = kn
        v_ref[...] = v
    sza_ref[...] = _silu(proj(w_za))
    sga_ref[...] = _sigmoid(proj(w_ga))


def _pre_call(x, norm_w, w_in_bf, conv_w, qw_t, kw_t, cb, seq_len, hist=None):
    n, d = x.shape
    is_prompt = hist is None
    n_cb = d // cb
    tm = 512 if is_prompt else n
    n_tiles = n // tm
    tiles_per_seq = seq_len // tm if is_prompt else 1

    tok = lambda j, i: (i, j)
    in_specs = [
        pl.BlockSpec((tm, d), lambda j, i: (i, 0)),
        pl.BlockSpec((1, d), lambda j, i: (0, 0)),
    ]
    for s in range(N_SPLITS):
        in_specs.append(pl.BlockSpec((d, cb), functools.partial(lambda j, i, s: (0, s * n_cb + j), s=s)))
    in_specs += [
        pl.BlockSpec((3, cb), lambda j, i: (0, j)),
        pl.BlockSpec((1, cb), lambda j, i: (0, 0)),
        pl.BlockSpec((1, cb), lambda j, i: (0, 0)),
        pl.BlockSpec((cb, cb), lambda j, i: (0, 0)),
    ]
    args = [x, norm_w, *([w_in_bf] * N_SPLITS), conv_w, qw_t, kw_t, _head_sum_matrix(cb)]
    if not is_prompt:
        in_specs += [pl.BlockSpec((tm, cb), tok), pl.BlockSpec((tm, cb), tok)]
        args += list(hist)

    if is_prompt:
        kv_shape = jax.ShapeDtypeStruct((n // seq_len, d, seq_len), F32)
        kv_spec = pl.BlockSpec((1, cb, tm), lambda j, i: (i // tiles_per_seq, j, i % tiles_per_seq))
        tail_shape = jax.ShapeDtypeStruct((n_tiles, CONV_TAIL, d), F32)
        tail_spec = pl.BlockSpec((1, CONV_TAIL, cb), lambda j, i: (i, 0, j))
    else:
        kv_shape = jax.ShapeDtypeStruct((n, d), F32)
        kv_spec = pl.BlockSpec((tm, cb), tok)
        tail_shape = jax.ShapeDtypeStruct((n, d), F32)
        tail_spec = pl.BlockSpec((tm, cb), tok)
    tok_spec = pl.BlockSpec((tm, cb), tok)
    out_shape = [jax.ShapeDtypeStruct((n, d), BF16), kv_shape, kv_shape,
                 jax.ShapeDtypeStruct((n, d), BF16),
                 jax.ShapeDtypeStruct((n, d), F32), jax.ShapeDtypeStruct((n, d), F32),
                 jax.ShapeDtypeStruct((n, d), F32), tail_shape]
    out_specs = [tok_spec, kv_spec, kv_spec, tok_spec, tok_spec, tok_spec, tok_spec, tail_spec]

    return pl.pallas_call(
        functools.partial(_pre_body, is_prompt, tm, cb, tiles_per_seq),
        grid=(n_cb, n_tiles),
        in_specs=in_specs,
        out_specs=out_specs,
        out_shape=out_shape,
        scratch_shapes=[pltpu.VMEM((CONV_TAIL, cb), F32)] if is_prompt else [],
        compiler_params=pltpu.CompilerParams(
            dimension_semantics=("arbitrary", "arbitrary"),
            vmem_limit_bytes=56 * 1024 * 1024),
        name="pre_prompt" if is_prompt else "pre_sample",
    )(*args)


def _attn_body(nt, pg, n_ss, t_new, pt_ref, q_ref, k_ref, v_ref, brow_ref, wm_ref,
               qs_ref, kn_ref, vn_ref, bs_ref, *rest):
    k_refs = rest[:pg]
    v_refs = rest[pg:2 * pg]
    o_ref, os_ref = rest[2 * pg:2 * pg + 2]
    kk_ref, vv_ref, qr_ref, kb_ref, vb_ref, c_ref, acc_ref = rest[2 * pg + 2:]
    t = nt * KEY_TILE
    d = qs_ref.shape[-1]
    half_id = pl.program_id(2)
    step = (pl.program_id(0) * pl.num_programs(1) + pl.program_id(1)) * 2 + half_id
    ss = step % n_ss
    wm = wm_ref[...]

    rows = N_HEADS * t_new
    row = lax.broadcasted_iota(jnp.int32, (rows, d), 0)
    lane = lax.broadcasted_iota(jnp.int32, (rows, d), 1)
    own_head = (row // t_new) == (lane // HEAD_DIM)
    bias_s = bs_ref[...]

    @pl.when(ss == 0)
    def _():
        qt = jnp.concatenate([qs_ref[...]] * N_HEADS, axis=0)
        qr = jnp.where(own_head, qt, 0.0).astype(BF16)
        qr_ref[...] = qr
        pad = jnp.zeros((PAIR - t_new, d), F32)
        kn = jnp.concatenate([kn_ref[...], pad], axis=0).astype(BF16)
        vn = jnp.concatenate([vn_ref[...], pad], axis=0).astype(BF16)
        z = lax.dot_general(qr, kn, _NT, preferred_element_type=F32) + bias_s
        rr = lax.broadcasted_iota(jnp.int32, (rows, PAIR), 0)
        ll = lax.broadcasted_iota(jnp.int32, (rows, PAIR), 1)
        mask = ll < (rr % t_new)
        a, c = _sb_pair(z, mask, jnp.zeros((rows, KEY_TILE), F32), wm)
        c_ref[...] = c
        acc_ref[...] = jnp.dot(a, vn, preferred_element_type=F32)

    group = kb_ref.shape[1] // KEY_TILE

    def sample_pages():
        c = c_ref[...]
        acc = acc_ref[...]
        for g in reversed(range(pg // group)):
            for p in range(group):
                lanes = slice(p * KEY_TILE, (p + 1) * KEY_TILE)
                kb_ref[:, lanes] = k_refs[g * group + p][0].reshape(d, KEY_TILE).astype(BF16)
                vb_ref[:, lanes] = v_refs[g * group + p][0].reshape(d, KEY_TILE).astype(BF16)
            z = jnp.dot(qr_ref[...], kb_ref[...], preferred_element_type=F32)
            a_pairs = [None] * (group // 2)
            for p in reversed(range(group // 2)):
                a_pairs[p], c = _sb_pair(z[:, p * PAIR:(p + 1) * PAIR] + bias_s, None, c, wm)
            acc = acc + lax.dot_general(jnp.concatenate(a_pairs, axis=1), vb_ref[...], _NT,
                                        preferred_element_type=F32)
        c_ref[...] = c
        acc_ref[...] = acc

    def prompt_prep():
        chan = lax.broadcasted_iota(jnp.int32, (2 * HEAD_DIM, t), 0)
        first = chan < HEAD_DIM
        k2 = k_ref[0]
        v2 = v_ref[0]
        k_h0 = jnp.where(first, k2, 0.0).astype(BF16)
        k_h1 = jnp.where(first, 0.0, k2).astype(BF16)
        v_h0 = jnp.where(first, v2, 0.0).astype(BF16)
        v_h1 = jnp.where(first, 0.0, v2).astype(BF16)
        b1 = brow_ref[0]
        brow = jnp.concatenate([b1[:, :KEY_TILE], b1[:, :KEY_TILE], b1[:, KEY_TILE:], b1[:, KEY_TILE:]],
                               axis=1)
        for p in range(nt // 2):
            cols = slice(p * PAIR, (p + 1) * PAIR)
            kk_ref[p, 0:KEY_TILE, 0:PAIR] = k_h0[:, cols]
            kk_ref[p, 0:KEY_TILE, PAIR:] = k_h1[:, cols]
            kk_ref[p, KEY_TILE:, :] = brow
            vv_ref[p, :, 0:PAIR] = v_h0[:, cols]
            vv_ref[p, :, PAIR:] = v_h1[:, cols]

    def pair(q_rows, p, mask, c):
        m = q_rows.shape[0]
        z4 = jnp.dot(q_rows, kk_ref[p], preferred_element_type=F32)
        z = jnp.concatenate([z4[:, :PAIR], z4[:, PAIR:]], axis=0)
        a, c = _sb_pair(z, mask, c, wm)
        a4 = jnp.concatenate([a[:m], a[m:]], axis=1)
        return lax.dot_general(a4, vv_ref[p], _NT, preferred_element_type=F32), c

    def band_mask(m):
        rr = lax.broadcasted_iota(jnp.int32, (m, PAIR), 0)
        cc = lax.broadcasted_iota(jnp.int32, (m, PAIR), 1)
        mk = cc < rr
        return jnp.concatenate([mk, mk], axis=0)

    hq = Q_CHUNK // 2

    def prompt_chunk(ic):
        r0 = ic * Q_CHUNK
        ones = jnp.ones((Q_CHUNK, KEY_TILE), BF16)
        q_c = jnp.concatenate([q_ref[0, r0:r0 + Q_CHUNK, :], ones], axis=1)
        zc = jnp.zeros((Q_CHUNK, KEY_TILE), F32)
        o_up, c_up = pair(q_c[hq:], 2 * ic + 1, band_mask(hq), zc)
        c0 = jnp.concatenate([zc[:hq], c_up[:hq], zc[:hq], c_up[hq:]], axis=0)
        o_c, cst = pair(q_c, 2 * ic, band_mask(Q_CHUNK), c0)
        o_c = o_c + jnp.concatenate([jnp.zeros((hq, 2 * HEAD_DIM), F32), o_up], axis=0)
        for p in reversed(range(2 * ic)):
            o_new, cst = pair(q_c, p, None, cst)
            o_c = o_c + o_new
        o_ref[0, r0:r0 + Q_CHUNK, :] = o_c

    n_chunks = t // Q_CHUNK

    @pl.when(half_id == 0)
    def _():
        prompt_prep()
        sample_pages()
        prompt_chunk(0)
        prompt_chunk(n_chunks - 1)

    @pl.when(half_id == 1)
    def _():
        sample_pages()
        for ic in range(1, n_chunks - 1):
            prompt_chunk(ic)

    @pl.when(ss == n_ss - 1)
    def _():
        m = jnp.where(own_head, acc_ref[...], 0.0)
        out = m[0:t_new, :]
        for hh in range(1, N_HEADS):
            out = out + m[hh * t_new:(hh + 1) * t_new, :]
        os_ref[...] = out


def _attn_call(qb, kt, vt, bias_rows_p, batch, seq_len,
               q_s, k_new, v_new, bias_rows_s, cache_k, cache_v, page_table, t_new):
    d = qb.shape[-1]
    n_hp = d // (2 * HEAD_DIM)
    nt = seq_len // KEY_TILE
    n_s = q_s.shape[0]
    dec_batch = n_s // t_new
    n_pages = page_table.shape[1]
    pg = PAGES_PER_STEP
    n_ss = n_pages // pg
    steps_per_batch = 2 * n_hp
    assert seq_len % Q_CHUNK == 0 and seq_len // Q_CHUNK >= 2 and Q_CHUNK == 2 * PAIR
    assert n_pages % pg == 0 and pg % SAMPLE_GROUP == 0 and SAMPLE_GROUP % 2 == 0
    assert batch * steps_per_batch == dec_batch * n_ss and steps_per_batch % n_ss == 0
    rows = N_HEADS * t_new

    def sample_pos(b, hp, hf):
        step = (b * n_hp + hp) * 2 + hf
        return step // n_ss, step % n_ss

    blk = pl.BlockSpec((1, seq_len, 2 * HEAD_DIM), lambda b, hp, hf, pt: (b, 0, hp))
    blk_t = pl.BlockSpec((1, 2 * HEAD_DIM, seq_len), lambda b, hp, hf, pt: (b, hp, 0))
    tok = pl.BlockSpec((t_new, d), lambda b, hp, hf, pt: (sample_pos(b, hp, hf)[0], 0))

    def page_spec(p):
        def index(b, hp, hf, pt):
            sb, ss = sample_pos(b, hp, hf)
            return (pt[sb, (n_ss - 1 - ss) * pg + p], 0, 0, 0)
        return pl.BlockSpec((1, N_HEADS, HEAD_DIM, KEY_TILE), index)

    grid_spec = pltpu.PrefetchScalarGridSpec(
        num_scalar_prefetch=1,
        grid=(batch, n_hp, 2),
        in_specs=[blk, blk_t, blk_t,
                  pl.BlockSpec((1, KEY_TILE, PAIR), lambda b, hp, hf, pt: (hp, 0, 0)),
                  pl.BlockSpec((PAIR, PAIR), lambda b, hp, hf, pt: (0, 0)),
                  tok, tok, tok,
                  pl.BlockSpec((rows, PAIR), lambda b, hp, hf, pt: (0, 0))]
                 + [page_spec(p) for p in range(pg)] * 2,
        out_specs=[blk, tok],
        scratch_shapes=[pltpu.VMEM((nt // 2, PAIR, 2 * PAIR), BF16),
                        pltpu.VMEM((nt // 2, 2 * HEAD_DIM, 2 * PAIR), BF16),
                        pltpu.VMEM((rows, d), BF16),
                        pltpu.VMEM((d, SAMPLE_GROUP * KEY_TILE), BF16),
                        pltpu.VMEM((d, SAMPLE_GROUP * KEY_TILE), BF16),
                        pltpu.VMEM((rows, KEY_TILE), F32),
                        pltpu.VMEM((rows, d), F32)],
    )
    return pl.pallas_call(
        functools.partial(_attn_body, nt, pg, n_ss, t_new),
        grid_spec=grid_spec,
        out_shape=[jax.ShapeDtypeStruct((batch, seq_len, d), F32),
                   jax.ShapeDtypeStruct((n_s, d), F32)],
        compiler_params=pltpu.CompilerParams(
            dimension_semantics=("arbitrary", "arbitrary", "arbitrary"),
            vmem_limit_bytes=58 * 1024 * 1024),
        name="attn",
    )(page_table, qb.reshape(batch, seq_len, d), kt, vt, bias_rows_p, _pair_suffix_matrix(),
      q_s, k_new, v_new, bias_rows_s, *([cache_k] * pg), *([cache_v] * pg))


def _post_body(o_ref, sza_ref, sga_ref, gc_ref, sgc_ref, x_ref, p_ref, woa_ref, woc_ref, wo_ref,
               wpg_ref, wpl_ref, pnw_ref, y_ref):
    ga = (o_ref[...] * sza_ref[...]).astype(BF16)
    y_a = jnp.dot(ga, woa_ref[...], preferred_element_type=F32)
    y_c = jnp.dot(gc_ref[...], woc_ref[...], preferred_element_type=F32)
    merged = sgc_ref[...] * y_c + sga_ref[...] * y_a
    x1 = x_ref[...] + jnp.dot(merged.astype(BF16), wo_ref[...], preferred_element_type=F32)
    ms = jnp.mean(x1 * x1, axis=-1, keepdims=True)
    nrm = (x1 * lax.rsqrt(ms + EPS) * pnw_ref[...]).astype(BF16)
    gate = _sigmoid(jnp.dot(nrm, wpg_ref[...], preferred_element_type=F32))
    ple = jnp.dot(p_ref[...].astype(BF16), wpl_ref[...], preferred_element_type=F32)
    y_ref[...] = x1 + ple * gate


def _post_call(o, sza, sga, gc, sgc, x, p, woa_bf, woc_bf, wo_bf, wpg_bf, wpl_bf, pnw):
    n, d = x.shape
    pd = p.shape[-1]
    tm = min(512, n)
    tokd = pl.BlockSpec((tm, d), lambda i: (i, 0))
    full = lambda shape: pl.BlockSpec(shape, lambda i: (0, 0))
    return pl.pallas_call(
        _post_body,
        grid=(n // tm,),
        in_specs=[tokd, tokd, tokd, tokd, tokd, tokd, pl.BlockSpec((tm, pd), lambda i: (i, 0)),
                  full((d, d)), full((d, d)), full((d, d)), full((d, d)), full((pd, d)), full((1, d))],
        out_specs=tokd,
        out_shape=jax.ShapeDtypeStruct((n, d), F32),
        compiler_params=pltpu.CompilerParams(
            dimension_semantics=("arbitrary",),
            vmem_limit_bytes=56 * 1024 * 1024),
        name="post",
    )(o, sza, sga, gc, sgc, x, p, woa_bf, woc_bf, wo_bf, wpg_bf, wpl_bf, pnw)


def kernel(x_prompt, x_sample, p_prompt, p_sample, cache_k, cache_v, state_conv, page_table, norm_w, w_in, conv_w, q_norm_w, k_norm_w, attn_bias, w_out_conv, w_out_attn, w_o, w_ple, ple_norm_w, w_ple_gate):
    depth = norm_w.shape[0]
    batch, seq_len, d = x_prompt.shape
    dec_batch, dec_seq, _ = x_sample.shape
    page = cache_k.shape[2]
    assert page == KEY_TILE and d == N_HEADS * HEAD_DIM and dec_seq == CONV_TAIL
    n_p = batch * seq_len
    n_s = dec_batch * dec_seq
    cb = 512

    hp = x_prompt.reshape(n_p, d)
    hs = x_sample.reshape(n_s, d)
    outs = [[] for _ in range(6)]
    for l in range(depth):
        w_in_bf = w_in[l].astype(BF16)
        woc_bf = w_out_conv[l].astype(BF16)
        woa_bf = w_out_attn[l].astype(BF16)
        wo_bf = w_o[l].astype(BF16)
        wpg_bf = w_ple_gate[l].astype(BF16)
        wpl_bf = w_ple[l].astype(BF16)
        nw = norm_w[l].reshape(1, d)
        pnw = ple_norm_w[l].reshape(1, d)
        qw_t = jnp.tile(q_norm_w[l], cb // HEAD_DIM).reshape(1, cb)
        kw_t = jnp.tile(k_norm_w[l], cb // HEAD_DIM).reshape(1, cb)
        bias2 = attn_bias[l] * LOG2E
        b_hi = bias2.astype(BF16)
        b_lo = (bias2 - b_hi.astype(F32)).astype(BF16)
        pair_lanes = lambda b: jnp.repeat(b.reshape(N_HEADS // 2, 2), KEY_TILE, axis=1)[:, None, :]
        bias_rows_p = jnp.concatenate(
            [pair_lanes(b_hi), pair_lanes(b_lo),
             jnp.zeros((N_HEADS // 2, KEY_TILE - 2, PAIR), BF16)], axis=1)
        bias_rows_s = jnp.broadcast_to(jnp.repeat(bias2, dec_seq)[:, None], (N_HEADS * dec_seq, PAIR))

        qb, kt, vt, gc, sgc, sza, sga, tails = _pre_call(
            hp, nw, w_in_bf, conv_w[l], qw_t, kw_t, cb, seq_len)
        st = state_conv[l]
        zrow = jnp.zeros((dec_batch, dec_seq, d), F32)
        hist1 = jnp.concatenate([st[:, 1:2], zrow[:, 1:]], axis=1).reshape(n_s, d)
        hist2 = jnp.concatenate([st[:, 0:2], zrow[:, 2:]], axis=1).reshape(n_s, d)
        qb_s, kf, vf, gc_s, sgc_s, sza_s, sga_s, u_s = _pre_call(
            hs, nw, w_in_bf, conv_w[l], qw_t, kw_t, cb, dec_seq, hist=(hist1, hist2))
        o, o_s = _attn_call(qb, kt, vt, bias_rows_p, batch, seq_len,
                            qb_s.astype(F32), kf, vf, bias_rows_s,
                            cache_k[l].transpose(0, 2, 3, 1), cache_v[l].transpose(0, 2, 3, 1),
                            page_table, dec_seq)
        hp = _post_call(o.reshape(n_p, d), sza, sga, gc, sgc, hp, p_prompt[l].reshape(n_p, -1),
                        woa_bf, woc_bf, wo_bf, wpg_bf, wpl_bf, pnw)
        hs = _post_call(o_s, sza_s, sga_s, gc_s, sgc_s, hs, p_sample[l].reshape(n_s, -1),
                        woa_bf, woc_bf, wo_bf, wpg_bf, wpl_bf, pnw)
        to_bthd = lambda a: a.reshape(batch, N_HEADS, HEAD_DIM, seq_len).transpose(0, 3, 1, 2)
        outs[0].append(to_bthd(kt))
        outs[1].append(to_bthd(vt))
        outs[2].append(tails.reshape(batch, -1, CONV_TAIL, d)[:, -1, CONV_TAIL - 2:, :])
        outs[3].append(kf.reshape(dec_batch, dec_seq, N_HEADS, HEAD_DIM))
        outs[4].append(vf.reshape(dec_batch, dec_seq, N_HEADS, HEAD_DIM))
        outs[5].append(u_s.reshape(dec_batch, dec_seq, d)[:, dec_seq - 2:, :])

    return (hp.reshape(batch, seq_len, d), hs.reshape(dec_batch, dec_seq, d),
            jnp.stack(outs[0]), jnp.stack(outs[1]), jnp.stack(outs[2]),
            jnp.stack(outs[3]), jnp.stack(outs[4]), jnp.stack(outs[5]))
```
